```python
import math
import jax, jax.numpy as jnp
from jax import lax
import numpy as np

D_MODEL = 2048
BATCH = 4
SEQ = 4096
DEPTH = 1

N_META = 16
BLOCK = 128
PAD = BLOCK - N_META

ATT_WIDTH = D_MODEL // 2
ATT_HEAD_DIM = 64
ATT_Q_HEADS = ATT_WIDTH // ATT_HEAD_DIM
ATT_KV_HEADS = 2
ATT_GROUP = ATT_Q_HEADS // ATT_KV_HEADS
ATT_KV_WIDTH = ATT_KV_HEADS * ATT_HEAD_DIM
WINDOW = 128
ROPE_THETA = 10000.0

RET_WIDTH = D_MODEL - ATT_WIDTH
RET_HEAD_DIM = 256
RET_HEADS = RET_WIDTH // RET_HEAD_DIM

MIX_WIDTH = ATT_WIDTH + RET_WIDTH
SPLITS = [ATT_WIDTH,
          ATT_WIDTH + ATT_KV_WIDTH,
          ATT_WIDTH + 2 * ATT_KV_WIDTH,
          ATT_WIDTH + 2 * ATT_KV_WIDTH + RET_WIDTH,
          ATT_WIDTH + 2 * ATT_KV_WIDTH + 2 * RET_WIDTH,
          ATT_WIDTH + 2 * ATT_KV_WIDTH + 3 * RET_WIDTH]
IN_COLS = ATT_WIDTH + 2 * ATT_KV_WIDTH + 4 * RET_WIDTH

PEER_HEADS = 8
PEER_N_KEYS = 128
PEER_N_EXPERTS = PEER_N_KEYS * PEER_N_KEYS
PEER_D_KEY = 256
PEER_HALF = PEER_D_KEY // 2
PEER_TOPK = 16
PEER_TOKEN_BLOCK = 128

DEEPNORM_ALPHA = (2 * DEPTH) ** 0.25
DEEPNORM_BETA = (8 * DEPTH) ** -0.25
LN_EPS = 1e-5
GN_EPS = 1e-5
NEG = -1e30

kernel_name = "hymba_swa_sink_retnet_peer_deepnorm"


def layer_norm(x, g, b):
    xf = x.astype(jnp.float32)
    mu = jnp.mean(xf, axis=-1, keepdims=True)
    var = jnp.mean(jnp.square(xf - mu), axis=-1, keepdims=True)
    y = (xf - mu) * lax.rsqrt(var + LN_EPS) * g.astype(jnp.float32) + b.astype(jnp.float32)
    return y.astype(x.dtype)


def rope(t, pos):
    dh = t.shape[-1]
    inv = ROPE_THETA ** (-jnp.arange(0, dh, 2, dtype=jnp.float32) / dh)
    ang = pos.astype(jnp.float32)[:, None] * inv[None, :]
    cos = jnp.cos(ang)[None, :, None, :]
    sin = jnp.sin(ang)[None, :, None, :]
    tf = t.astype(jnp.float32)
    t1, t2 = tf[..., : dh // 2], tf[..., dh // 2:]
    return jnp.concatenate([t1 * cos - t2 * sin, t2 * cos + t1 * sin], axis=-1).astype(t.dtype)


def sliding_window_attention(q, k, v, sinks):
    B, Lp = q.shape[0], q.shape[1]
    nb = Lp // BLOCK
    qb = q.reshape(B, nb, BLOCK, ATT_KV_HEADS, ATT_GROUP, ATT_HEAD_DIM).astype(jnp.float32)

    def with_prev(t):
        tb = t.reshape(B, nb, BLOCK, ATT_KV_HEADS, ATT_HEAD_DIM).astype(jnp.float32)
        prev = jnp.concatenate([jnp.zeros_like(tb[:, :1]), tb[:, :-1]], axis=1)
        return jnp.concatenate([prev, tb], axis=2)

    kb, vb = with_prev(k), with_prev(v)
    scores = jnp.einsum('bnqhgd,bnkhd->bnhgqk', qb, kb) * (ATT_HEAD_DIM ** -0.5)
    blk = jnp.arange(nb)[:, None] * BLOCK
    qpos = blk + jnp.arange(BLOCK)[None, :]
    kpos = blk - BLOCK + jnp.arange(2 * BLOCK)[None, :]
    diff = qpos[:, :, None] - kpos[:, None, :]
    mask = (diff >= 0) & (diff < WINDOW) & (kpos[:, None, :] >= PAD)
    scores = jnp.where(mask[None, :, None, None], scores, NEG)
    sink = sinks.astype(jnp.float32).reshape(1, 1, ATT_KV_HEADS, ATT_GROUP, 1, 1)
    m = jnp.maximum(jnp.max(scores, axis=-1, keepdims=True), sink)
    p = jnp.exp(scores - m)
    denom = jnp.sum(p, axis=-1, keepdims=True) + jnp.exp(sink - m)
    p = p / denom
    out = jnp.einsum('bnhgqk,bnkhd->bnqhgd', p, vb)
    return out.reshape(B, Lp, ATT_Q_HEADS * ATT_HEAD_DIM).astype(q.dtype)


def chunkwise_retention(q, k, v):
    B, Lp = q.shape[0], q.shape[1]
    nb = Lp // BLOCK
    log_gamma = jnp.log(1.0 - 2.0 ** (-5.0 - jnp.arange(RET_HEADS, dtype=jnp.float32)))
    idx = jnp.arange(BLOCK, dtype=jnp.float32)
    rel = idx[:, None] - idx[None, :]
    inner_decay = jnp.where(rel[None] >= 0,
                            jnp.exp(jnp.maximum(rel, 0.0)[None] * log_gamma[:, None, None]), 0.0)
    q_decay = jnp.exp((idx[None, :] + 1.0) * log_gamma[:, None])[:, :, None]
    k_decay = jnp.exp((BLOCK - 1.0 - idx[None, :]) * log_gamma[:, None])[:, :, None]
    chunk_decay = jnp.exp(BLOCK * log_gamma)[:, None, None]

    def to_chunks(t):
        return t.astype(jnp.float32).reshape(B, nb, BLOCK, RET_HEADS, t.shape[-1]).transpose(1, 0, 3, 2, 4)

    qs, ks, vs = to_chunks(q), to_chunks(k * (RET_HEAD_DIM ** -0.5)), to_chunks(v)

    def step(state, inp):
        qc, kc, vc = inp
        att = jnp.einsum('bhid,bhjd->bhij', qc, kc) * inner_decay
        o = jnp.einsum('bhij,bhje->bhie', att, vc) + jnp.einsum('bhid,bhde->bhie', qc * q_decay, state)
        state = state * chunk_decay + jnp.einsum('bhjd,bhje->bhde', kc * k_decay, vc)
        return state, o

    state0 = jnp.zeros((B, RET_HEADS, RET_HEAD_DIM, v.shape[-1]), jnp.float32)
    _, o = lax.scan(step, state0, (qs, ks, vs))
    return o.transpose(1, 0, 3, 2, 4).reshape(B, Lp, RET_HEADS, v.shape[-1])


def hybrid_mixer(h, w_in, sinks, w_out, pos, key_valid):
    B, Lp, _ = h.shape
    proj = h @ w_in
    aq, ak, av, rq, rk, rv, rg = jnp.split(proj, SPLITS, axis=-1)
    aq = rope(aq.reshape(B, Lp, ATT_Q_HEADS, ATT_HEAD_DIM), pos)
    ak = rope(ak.reshape(B, Lp, ATT_KV_HEADS, ATT_HEAD_DIM), pos)
    av = av.reshape(B, Lp, ATT_KV_HEADS, ATT_HEAD_DIM)
    att_out = sliding_window_attention(aq, ak, av, sinks)
    rq = rope(rq.reshape(B, Lp, RET_HEADS, RET_HEAD_DIM), pos)
    rk = rope((rk * key_valid).reshape(B, Lp, RET_HEADS, RET_HEAD_DIM), pos)
    rv = rv.reshape(B, Lp, RET_HEADS, RET_HEAD_DIM)
    ret = chunkwise_retention(rq, rk, rv)
    mu = jnp.mean(ret, axis=-1, keepdims=True)
    var = jnp.mean(jnp.square(ret - mu), axis=-1, keepdims=True)
    ret = ((ret - mu) * lax.rsqrt(var + GN_EPS)).reshape(B, Lp, RET_WIDTH)
    ret_out = (jax.nn.silu(rg.astype(jnp.float32)) * ret).astype(h.dtype)
    mixed = jnp.concatenate([att_out, ret_out], axis=-1)
    return mixed @ w_out


def peer(h, wq, subkeys, u_tab, v_tab):
    B, Lp, D = h.shape
    T = B * Lp
    t = h.reshape(T, D)
    q = (t @ wq).reshape(T, PEER_HEADS, 2, PEER_HALF).astype(jnp.float32)
    s = jnp.einsum('thpc,hpnc->thpn', q, subkeys.astype(jnp.float32))
    sv, si = lax.top_k(s, PEER_TOPK)
    cand = (sv[:, :, 0, :, None] + sv[:, :, 1, None, :]).reshape(T, PEER_HEADS, PEER_TOPK * PEER_TOPK)
    fv, fi = lax.top_k(cand, PEER_TOPK)
    i1 = jnp.take_along_axis(si[:, :, 0], fi // PEER_TOPK, axis=-1)
    i2 = jnp.take_along_axis(si[:, :, 1], fi % PEER_TOPK, axis=-1)
    ids = (i1 * PEER_N_KEYS + i2).reshape(T, PEER_HEADS * PEER_TOPK)
    gates = jax.nn.softmax(fv, axis=-1).reshape(T, PEER_HEADS * PEER_TOPK)
    nblk = T // PEER_TOKEN_BLOCK

    def expert_block(args):
        tb, idb, gb = args
        u = jnp.take(u_tab, idb, axis=0)
        a = jax.nn.gelu(jnp.einsum('td,ted->te', tb, u).astype(jnp.float32), approximate=False)
        vv = jnp.take(v_tab, idb, axis=0)
        return jnp.einsum('te,ted->td', (gb * a).astype(vv.dtype), vv)

    out = lax.map(expert_block, (t.reshape(nblk, PEER_TOKEN_BLOCK, D),
                                 ids.reshape(nblk, PEER_TOKEN_BLOCK, -1),
                                 gates.reshape(nblk, PEER_TOKEN_BLOCK, -1)))
    return out.reshape(B, Lp, D).astype(h.dtype)


def setup_inputs(seed: int = 0) -> dict:
    key = jax.random.key(seed)
    ks = jax.random.split(key, 16)
    nrm = lambda k, shape: jax.random.normal(k, shape, jnp.float32)
    x = nrm(ks[0], (BATCH, SEQ, D_MODEL))
    meta_tokens = nrm(ks[1], (N_META, D_MODEL))
    ln_in_g = 1.0 + 0.02 * nrm(ks[2], (D_MODEL,))
    ln_in_b = 0.02 * nrm(ks[3], (D_MODEL,))
    beta = DEEPNORM_BETA
    col_scale = jnp.concatenate([
        jnp.ones((ATT_WIDTH + ATT_KV_WIDTH,), jnp.float32),
        jnp.full((ATT_KV_WIDTH,), beta, jnp.float32),
        jnp.ones((2 * RET_WIDTH,), jnp.float32),
        jnp.full((RET_WIDTH,), beta, jnp.float32),
        jnp.ones((RET_WIDTH,), jnp.float32)])
    w_in = nrm(ks[4], (DEPTH, D_MODEL, IN_COLS)) * (D_MODEL ** -0.5) * col_scale
    att_sinks = 0.5 * nrm(ks[5], (DEPTH, ATT_Q_HEADS))
    w_out = nrm(ks[6], (DEPTH, MIX_WIDTH, D_MODEL)) * (MIX_WIDTH ** -0.5) * beta
    ln_mix_g = 1.0 + 0.02 * nrm(ks[7], (DEPTH, D_MODEL))
    ln_mix_b = 0.02 * nrm(ks[8], (DEPTH, D_MODEL))
    peer_wq = nrm(ks[9], (DEPTH, D_MODEL, PEER_HEADS * PEER_D_KEY)) * (D_MODEL ** -0.5)
    peer_subkeys = nrm(ks[10], (DEPTH, PEER_HEADS, 2, PEER_N_KEYS, PEER_HALF)) * (PEER_HALF ** -0.5)
    peer_u = nrm(ks[11], (DEPTH, PEER_N_EXPERTS, D_MODEL)) * (D_MODEL ** -0.5)
    peer_v = nrm(ks[12], (DEPTH, PEER_N_EXPERTS, D_MODEL)) * beta
    ln_ffn_g = 1.0 + 0.02 * nrm(ks[13], (DEPTH, D_MODEL))
    ln_ffn_b = 0.02 * nrm(ks[14], (DEPTH, D_MODEL))
    return {"x": x, "meta_tokens": meta_tokens, "ln_in_g": ln_in_g, "ln_in_b": ln_in_b,
            "w_in": w_in, "att_sinks": att_sinks, "w_out": w_out,
            "ln_mix_g": ln_mix_g, "ln_mix_b": ln_mix_b,
            "peer_wq": peer_wq, "peer_subkeys": peer_subkeys, "peer_u": peer_u, "peer_v": peer_v,
            "ln_ffn_g": ln_ffn_g, "ln_ffn_b": ln_ffn_b}


def reference(x, meta_tokens, ln_in_g, ln_in_b, w_in, att_sinks, w_out, ln_mix_g, ln_mix_b,
              peer_wq, peer_subkeys, peer_u, peer_v, ln_ffn_g, ln_ffn_b):
    B = x.shape[0]
    pad = jnp.zeros((B, PAD, D_MODEL), x.dtype)
    meta = jnp.broadcast_to(meta_tokens.astype(x.dtype)[None], (B, N_META, D_MODEL))
    h = jnp.concatenate([pad, meta, x], axis=1)
    Lp = h.shape[1]
    slot = jnp.arange(Lp, dtype=jnp.int32)
    pos = slot - PAD
    key_valid = (slot >= PAD)[None, :, None].astype(x.dtype)
    h = layer_norm(h, ln_in_g, ln_in_b)
    for l in range(DEPTH):
        mix = hybrid_mixer(h, w_in[l], att_sinks[l], w_out[l], pos, key_valid)
        h = layer_norm(DEEPNORM_ALPHA * h + mix, ln_mix_g[l], ln_mix_b[l])
        ffn = peer(h, peer_wq[l], peer_subkeys[l], peer_u[l], peer_v[l])
        h = layer_norm(DEEPNORM_ALPHA * h + ffn, ln_ffn_g[l], ln_ffn_b[l])
    return h[:, BLOCK:]
```

```python
import functools

import jax
import jax.numpy as jnp
from jax import lax
from jax.experimental import pallas as pl
from jax.experimental.pallas import tpu as pltpu

F32 = jnp.float32
BF16 = jnp.bfloat16

D_MODEL = 2048
DEPTH = 1
N_META = 16
BLOCK = 128
PAD = BLOCK - N_META

ATT_WIDTH = 1024
ATT_HEAD_DIM = 64
ATT_Q_HEADS = 16
ATT_KV_HEADS = 2
ATT_GROUP = 8
ATT_KV_WIDTH = 128
ROPE_THETA = 10000.0

RET_WIDTH = 1024
RET_HEAD_DIM = 256
RET_HEADS = 4

IN_COLS = ATT_WIDTH + 2 * ATT_KV_WIDTH + 4 * RET_WIDTH
COL_AK = (ATT_WIDTH) // ATT_KV_WIDTH
COL_AV = (ATT_WIDTH + ATT_KV_WIDTH) // ATT_KV_WIDTH
RET_BASE = ATT_WIDTH + 2 * ATT_KV_WIDTH
COL_RQ = RET_BASE // RET_HEAD_DIM
COL_RK = (RET_BASE + RET_WIDTH) // RET_HEAD_DIM
COL_RV = (RET_BASE + 2 * RET_WIDTH) // RET_HEAD_DIM
COL_RG = (RET_BASE + 3 * RET_WIDTH) // RET_HEAD_DIM

PEER_HEADS = 8
PEER_N_KEYS = 128
PEER_N_EXPERTS = PEER_N_KEYS * PEER_N_KEYS
PEER_HALF = 128
PEER_TOPK = 16
PEER_SEL = PEER_HEADS * PEER_TOPK

DEEPNORM_ALPHA = (2 * DEPTH) ** 0.25
LN_EPS = 1e-5
GN_EPS = 1e-5
NEG = -1e30

VMEM_LIMIT_BYTES = 56 * 1024 * 1024

INPROJ_TM = 512
INPROJ_TN = 768
MIX_TM = 256
ROUTE_TM = 256
EXPERT_TT = 256
EXPERT_EB = 512
GATE_PITCH_PAD = 8


def _layer_norm(x, g, b):
    mu = jnp.mean(x, axis=-1, keepdims=True)
    xc = x - mu
    var = jnp.mean(xc * xc, axis=-1, keepdims=True)
    return xc * lax.rsqrt(var + LN_EPS) * g + b


def _cparams(sem):
    return pltpu.CompilerParams(dimension_semantics=sem, vmem_limit_bytes=VMEM_LIMIT_BYTES)


def _inproj_kernel(x_ref, g_ref, b_ref, w_ref, h_ref, proj_ref, hb_ref):
    @pl.when(pl.program_id(1) == 0)
    def _():
        h = _layer_norm(x_ref[...], g_ref[...], b_ref[...])
        h_ref[...] = h
        hb_ref[...] = h.astype(BF16)

    proj_ref[...] = jnp.dot(hb_ref[...], w_ref[...], preferred_element_type=F32)


def _inproj(x2d, g, b, w_bf16, tm):
    t = x2d.shape[0]
    tn = INPROJ_TN
    return pl.pallas_call(
        _inproj_kernel,
        out_shape=(jax.ShapeDtypeStruct((t, D_MODEL), F32),
                   jax.ShapeDtypeStruct((t, IN_COLS), F32)),
        grid=(t // tm, IN_COLS // tn),
        in_specs=[pl.BlockSpec((tm, D_MODEL), lambda i, j: (i, 0)),
                  pl.BlockSpec((1, D_MODEL), lambda i, j: (0, 0)),
                  pl.BlockSpec((1, D_MODEL), lambda i, j: (0, 0)),
                  pl.BlockSpec((D_MODEL, tn), lambda i, j: (0, j))],
        out_specs=(pl.BlockSpec((tm, D_MODEL), lambda i, j: (i, 0)),
                   pl.BlockSpec((tm, tn), lambda i, j: (i, j))),
        scratch_shapes=[pltpu.VMEM((tm, D_MODEL), BF16)],
        compiler_params=_cparams(("parallel", "arbitrary")),
        name="inproj",
    )(x2d, g, b, w_bf16)


def _att_rope(x, cos, sin_signed, lo_half):
    partner = jnp.where(lo_half, pltpu.roll(x, 96, 1), pltpu.roll(x, 32, 1))
    return x * cos + partner * sin_signed


def _att_kernel(sink_ref, q_ref, kc_ref, vc_ref, kp_ref, vp_ref, km_ref, vm_ref,
                cq_ref, sq_ref, cp_ref, sp_ref, o_ref):
    first = pl.program_id(1) == 0
    lane = lax.broadcasted_iota(jnp.int32, (BLOCK, 128), 1)
    lo_half = (lane % ATT_HEAD_DIM) < (ATT_HEAD_DIM // 2)
    cq, sq = cq_ref[...], sq_ref[...]

    k_prev = jnp.where(first, km_ref[...], kp_ref[...])
    v_prev = jnp.where(first, vm_ref[...], vp_ref[...])
    k_prev = _att_rope(k_prev, cp_ref[...], sp_ref[...], lo_half)
    k_cur = _att_rope(kc_ref[...], cq, sq, lo_half)
    k_all = jnp.concatenate([k_prev, k_cur], axis=0).astype(BF16)
    v_all = jnp.concatenate([v_prev, vc_ref[...]], axis=0).astype(BF16)

    qi = lax.broadcasted_iota(jnp.int32, (BLOCK, 2 * BLOCK), 0)
    kj = lax.broadcasted_iota(jnp.int32, (BLOCK, 2 * BLOCK), 1)
    prev_ok = (kj > qi) & (jnp.logical_not(first) | (kj >= PAD))
    mask = ((kj < BLOCK) & prev_ok) | ((kj >= BLOCK) & ((kj - BLOCK) <= qi))

    for c in range(ATT_Q_HEADS // 2):
        qc = _att_rope(q_ref[:, c * 128:(c + 1) * 128], cq, sq, lo_half).astype(BF16)
        for hh in range(2):
            h = 2 * c + hh
            g = h // ATT_GROUP
            qh = qc[:, hh * ATT_HEAD_DIM:(hh + 1) * ATT_HEAD_DIM]
            kg = k_all[:, g * ATT_HEAD_DIM:(g + 1) * ATT_HEAD_DIM]
            vg = v_all[:, g * ATT_HEAD_DIM:(g + 1) * ATT_HEAD_DIM]
            s = lax.dot_general(qh, kg, (((1,), (1,)), ((), ())),
                                preferred_element_type=F32) * (ATT_HEAD_DIM ** -0.5)
            s = jnp.where(mask, s, NEG)
            sink = sink_ref[h]
            m = jnp.maximum(jnp.max(s, axis=-1, keepdims=True), sink)
            p = jnp.exp(s - m)
            denom = jnp.sum(p, axis=-1, keepdims=True) + jnp.exp(sink - m)
            o = jnp.dot(p.astype(BF16), vg, preferred_element_type=F32) / denom
            o_ref[:, h * ATT_HEAD_DIM:(h + 1) * ATT_HEAD_DIM] = o.astype(BF16)


def _attention(proj, proj_meta, sinks, cos_a, sin_a, batch, nbk):
    t = proj.shape[0]

    def row(b, n):
        return b * nbk + n

    def prow(b, n):
        return b * nbk + jnp.maximum(n - 1, 0)

    blk = (BLOCK, ATT_KV_WIDTH)
    return pl.pallas_call(
        _att_kernel,
        out_shape=jax.ShapeDtypeStruct((t, ATT_WIDTH), BF16),
        grid=(batch, nbk),
        in_specs=[pl.BlockSpec(memory_space=pltpu.SMEM),
                  pl.BlockSpec((BLOCK, ATT_WIDTH), lambda b, n: (row(b, n), 0)),
                  pl.BlockSpec(blk, lambda b, n: (row(b, n), COL_AK)),
                  pl.BlockSpec(blk, lambda b, n: (row(b, n), COL_AV)),
                  pl.BlockSpec(blk, lambda b, n: (prow(b, n), COL_AK)),
                  pl.BlockSpec(blk, lambda b, n: (prow(b, n), COL_AV)),
                  pl.BlockSpec(blk, lambda b, n: (0, COL_AK)),
                  pl.BlockSpec(blk, lambda b, n: (0, COL_AV)),
                  pl.BlockSpec(blk, lambda b, n: (n + 1, 0)),
                  pl.BlockSpec(blk, lambda b, n: (n + 1, 0)),
                  pl.BlockSpec(blk, lambda b, n: (n, 0)),
                  pl.BlockSpec(blk, lambda b, n: (n, 0))],
        out_specs=pl.BlockSpec((BLOCK, ATT_WIDTH), lambda b, n: (row(b, n), 0)),
        compiler_params=_cparams(("parallel", "arbitrary")),
        name="attention",
    )(sinks, proj, proj, proj, proj, proj, proj_meta, proj_meta, cos_a, sin_a, cos_a, sin_a)


def _ret_rope(x, cos, sin_signed):
    partner = jnp.concatenate([x[:, RET_HEAD_DIM // 2:], x[:, :RET_HEAD_DIM // 2]], axis=1)
    return x * cos + partner * sin_signed


def _ret_kernel(cd_ref, q_ref, k_ref, v_ref, g_ref, km_ref, vm_ref, cq_ref, sq_ref, cm_ref, sm_ref,
                inner_ref, qd_ref, kd_ref, o_ref, state_ref):
    h = pl.program_id(1)
    kscale = RET_HEAD_DIM ** -0.5
    kd = kd_ref[...]

    @pl.when(pl.program_id(2) == 0)
    def _():
        valid = lax.broadcasted_iota(jnp.int32, (BLOCK, 1), 0) >= PAD
        km = _ret_rope(jnp.where(valid, km_ref[...], 0.0), cm_ref[...], sm_ref[...]) * kscale
        state_ref[...] = lax.dot_general((km * kd).astype(BF16), vm_ref[...].astype(BF16),
                                         (((0,), (0,)), ((), ())), preferred_element_type=F32)

    cq, sq = cq_ref[...], sq_ref[...]
    q = _ret_rope(q_ref[...], cq, sq)
    k = _ret_rope(k_ref[...], cq, sq) * kscale
    vb = v_ref[...].astype(BF16)
    state = state_ref[...]

    att = lax.dot_general(q.astype(BF16), k.astype(BF16), (((1,), (1,)), ((), ())),
                          preferred_element_type=F32) * inner_ref[...]
    o = (jnp.dot(att.astype(BF16), vb, preferred_element_type=F32)
         + jnp.dot((q * qd_ref[...]).astype(BF16), state.astype(BF16), preferred_element_type=F32))
    state_ref[...] = state * cd_ref[h] + lax.dot_general(
        (k * kd).astype(BF16), vb, (((0,), (0,)), ((), ())), preferred_element_type=F32)

    mu = jnp.mean(o, axis=-1, keepdims=True)
    oc = o - mu
    var = jnp.mean(oc * oc, axis=-1, keepdims=True)
    gate = g_ref[...]
    silu = gate * (1.0 / (1.0 + jnp.exp(-gate)))
    o_ref[...] = (silu * (oc * lax.rsqrt(var + GN_EPS))).astype(BF16)


def _retention(proj, proj_meta, cos_r, sin_r, inner, qdec, kdec, chunk_decay, batch, nbk):
    t = proj.shape[0]
    blk = (BLOCK, RET_HEAD_DIM)

    def row(b, h, n):
        return b * nbk + n

    return pl.pallas_call(
        _ret_kernel,
        out_shape=jax.ShapeDtypeStruct((t, RET_WIDTH), BF16),
        grid=(batch, RET_HEADS, nbk),
        in_specs=[pl.BlockSpec(memory_space=pltpu.SMEM),
                  pl.BlockSpec(blk, lambda b, h, n: (row(b, h, n), COL_RQ + h)),
                  pl.BlockSpec(blk, lambda b, h, n: (row(b, h, n), COL_RK + h)),
                  pl.BlockSpec(blk, lambda b, h, n: (row(b, h, n), COL_RV + h)),
                  pl.BlockSpec(blk, lambda b, h, n: (row(b, h, n), COL_RG + h)),
                  pl.BlockSpec(blk, lambda b, h, n: (0, COL_RK + h)),
                  pl.BlockSpec(blk, lambda b, h, n: (0, COL_RV + h)),
                  pl.BlockSpec(blk, lambda b, h, n: (n + 1, 0)),
                  pl.BlockSpec(blk, lambda b, h, n: (n + 1, 0)),
                  pl.BlockSpec(blk, lambda b, h, n: (0, 0)),
                  pl.BlockSpec(blk, lambda b, h, n: (0, 0)),
                  pl.BlockSpec((None, BLOCK, BLOCK), lambda b, h, n: (h, 0, 0)),
                  pl.BlockSpec((None, BLOCK, 1), lambda b, h, n: (h, 0, 0)),
                  pl.BlockSpec((None, BLOCK, 1), lambda b, h, n: (h, 0, 0))],
        out_specs=pl.BlockSpec(blk, lambda b, h, n: (row(b, h, n), h)),
        scratch_shapes=[pltpu.VMEM((RET_HEAD_DIM, RET_HEAD_DIM), F32)],
        compiler_params=_cparams(("parallel", "parallel", "arbitrary")),
        name="retention",
    )(chunk_decay, proj, proj, proj, proj, proj_meta, proj_meta, cos_r, sin_r, cos_r, sin_r,
      inner, qdec, kdec)


def _mix_kernel(att_ref, ret_ref, h_ref, w_ref, g_ref, b_ref, h2_ref, h2b_ref):
    mix = (jnp.dot(att_ref[...], w_ref[:ATT_WIDTH, :], preferred_element_type=F32)
           + jnp.dot(ret_ref[...], w_ref[ATT_WIDTH:, :], preferred_element_type=F32))
    h2 = _layer_norm(DEEPNORM_ALPHA * h_ref[...] + mix, g_ref[...], b_ref[...])
    h2_ref[...] = h2
    h2b_ref[...] = h2.astype(BF16)


def _mix(att, ret, h, w_bf16, g, b, tm):
    t = h.shape[0]
    return pl.pallas_call(
        _mix_kernel,
        out_shape=(jax.ShapeDtypeStruct((t, D_MODEL), F32),
                   jax.ShapeDtypeStruct((t, D_MODEL), BF16)),
        grid=(t // tm,),
        in_specs=[pl.BlockSpec((tm, ATT_WIDTH), lambda i: (i, 0)),
                  pl.BlockSpec((tm, RET_WIDTH), lambda i: (i, 0)),
                  pl.BlockSpec((tm, D_MODEL), lambda i: (i, 0)),
                  pl.BlockSpec((D_MODEL, D_MODEL), lambda i: (0, 0)),
                  pl.BlockSpec((1, D_MODEL), lambda i: (0, 0)),
                  pl.BlockSpec((1, D_MODEL), lambda i: (0, 0))],
        out_specs=(pl.BlockSpec((tm, D_MODEL), lambda i: (i, 0)),
                   pl.BlockSpec((tm, D_MODEL), lambda i: (i, 0))),
        compiler_params=_cparams(("parallel",)),
        name="mix",
    )(att, ret, h, w_bf16, g, b)


def _top16_rows(vals, n_rows, emit):
    rows = lax.broadcasted_iota(jnp.int32, vals.shape, 0)
    for k in range(PEER_TOPK):
        m = jnp.max(vals, axis=0, keepdims=True)
        idx = jnp.min(jnp.where(vals == m, rows, n_rows), axis=0, keepdims=True)
        picked = rows == idx
        emit(k, m, idx, picked)
        vals = jnp.where(picked, -jnp.inf, vals)


def _route_kernel(hb_ref, wq_ref, sk_ref, ids_ref, gates_ref,
                  q_scr, sv_scr, si_scr, cand_scr, idc_scr, ids_t, gates_t):
    q_scr[...] = jnp.dot(hb_ref[...], wq_ref[...], preferred_element_type=F32)

    def head_body(h, carry):
        for p in range(2):
            off = pl.multiple_of(h * (2 * PEER_HALF) + p * PEER_HALF, PEER_HALF)
            qhp = q_scr[:, pl.ds(off, PEER_HALF)].astype(BF16)
            s = lax.dot_general(sk_ref[h, p], qhp, (((1,), (1,)), ((), ())),
                                preferred_element_type=F32)

            def emit1(k, m, idx, picked, p=p):
                sv_scr[p, k:k + 1, :] = m
                si_scr[p, k:k + 1, :] = idx

            _top16_rows(s, PEER_N_KEYS, emit1)

        sv1 = sv_scr[1]
        si1 = si_scr[1]
        for a in range(PEER_TOPK):
            cand_scr[a * PEER_TOPK:(a + 1) * PEER_TOPK, :] = sv_scr[0, a:a + 1, :] + sv1
            idc_scr[a * PEER_TOPK:(a + 1) * PEER_TOPK, :] = si_scr[0, a:a + 1, :] * PEER_N_KEYS + si1

        idc = idc_scr[...]
        base = pl.multiple_of(h * PEER_TOPK, PEER_TOPK)

        def emit2(k, m, idx, picked):
            sv_scr[0, k:k + 1, :] = m
            ids_t[pl.ds(base + k, 1), :] = jnp.max(jnp.where(picked, idc, -1), axis=0, keepdims=True)

        _top16_rows(cand_scr[...], PEER_TOPK * PEER_TOPK, emit2)

        fv = sv_scr[0]
        e = jnp.exp(fv - fv[0:1, :])
        gates_t[pl.ds(base, PEER_TOPK), :] = e / jnp.sum(e, axis=0, keepdims=True)
        return carry

    lax.fori_loop(0, PEER_HEADS, head_body, 0)
    ids_ref[...] = ids_t[...].T
    gates_ref[...] = gates_t[...].T


def _route(h2b, wq_bf16, sk_bf16, tm):
    t = h2b.shape[0]
    return pl.pallas_call(
        _route_kernel,
        out_shape=(jax.ShapeDtypeStruct((t, PEER_SEL), jnp.int32),
                   jax.ShapeDtypeStruct((t, PEER_SEL), F32)),
        grid=(t // tm,),
        in_specs=[pl.BlockSpec((tm, D_MODEL), lambda i: (i, 0)),
                  pl.BlockSpec((D_MODEL, PEER_HEADS * 2 * PEER_HALF), lambda i: (0, 0)),
                  pl.BlockSpec((PEER_HEADS, 2, PEER_N_KEYS, PEER_HALF), lambda i: (0, 0, 0, 0))],
        out_specs=(pl.BlockSpec((tm, PEER_SEL), lambda i: (i, 0)),
                   pl.BlockSpec((tm, PEER_SEL), lambda i: (i, 0))),
        scratch_shapes=[pltpu.VMEM((tm, PEER_HEADS * 2 * PEER_HALF), F32),
                        pltpu.VMEM((2, PEER_TOPK, tm), F32),
                        pltpu.VMEM((2, PEER_TOPK, tm), jnp.int32),
                        pltpu.VMEM((PEER_TOPK * PEER_TOPK, tm), F32),
                        pltpu.VMEM((PEER_TOPK * PEER_TOPK, tm), jnp.int32),
                        pltpu.VMEM((PEER_SEL, tm), jnp.int32),
                        pltpu.VMEM((PEER_SEL, tm), F32)],
        compiler_params=_cparams(("parallel",)),
        name="route",
    )(h2b, wq_bf16, sk_bf16)


def _gelu_exact(x):
    return 0.5 * x * (1.0 + lax.erf(x * (2.0 ** -0.5)))


def _expert_kernel(ids_ref, gates_ref, hb_ref, h2_ref, u_ref, v_ref, g_ref, b_ref, o_ref,
                   wg_scr, gact_scr, acc_scr, *, tt, eb):
    e = pl.program_id(1)
    pitch = tt + GATE_PITCH_PAD
    n_sub = eb // PEER_N_KEYS

    @pl.when(e == 0)
    def _():
        acc_scr[...] = jnp.zeros_like(acc_scr)
        sub = lax.broadcasted_iota(jnp.int32, (PEER_N_KEYS, PEER_SEL), 0)

        def token_body(t, carry):
            ids = ids_ref[pl.ds(t, 1), :]
            gates = gates_ref[pl.ds(t, 1), :]
            left = jnp.where(sub == (ids // PEER_N_KEYS), gates, 0.0).astype(BF16)
            right = jnp.where(sub == (ids % PEER_N_KEYS), 1.0, 0.0).astype(BF16)
            w_t = lax.dot_general(left, right, (((1,), (1,)), ((), ())),
                                  preferred_element_type=F32)
            wg_scr[pl.ds(t, PEER_N_KEYS, stride=pitch), :] = w_t
            return carry

        lax.fori_loop(0, tt, token_body, 0)

    a = lax.dot_general(hb_ref[...], u_ref[...], (((1,), (1,)), ((), ())),
                        preferred_element_type=F32)
    for c in range(n_sub):
        start = pl.multiple_of((e * n_sub + c) * pitch, 8)
        wg = wg_scr[pl.ds(start, tt), :]
        ac = a[:, c * PEER_N_KEYS:(c + 1) * PEER_N_KEYS]
        gact_scr[:, c * PEER_N_KEYS:(c + 1) * PEER_N_KEYS] = (_gelu_exact(ac) * wg).astype(BF16)
    acc_scr[...] += jnp.dot(gact_scr[...], v_ref[...], preferred_element_type=F32)

    @pl.when(e == pl.num_programs(1) - 1)
    def _():
        o_ref[...] = _layer_norm(DEEPNORM_ALPHA * h2_ref[...] + acc_scr[...], g_ref[...], b_ref[...])


def _experts(ids, gates, h2b, h2, u_bf16, v_bf16, g, b, tt, eb):
    t = h2.shape[0]
    pitch = tt + GATE_PITCH_PAD
    return pl.pallas_call(
        functools.partial(_expert_kernel, tt=tt, eb=eb),
        out_shape=jax.ShapeDtypeStruct((t, D_MODEL), F32),
        grid=(t // tt, PEER_N_EXPERTS // eb),
        in_specs=[pl.BlockSpec((tt, PEER_SEL), lambda i, e: (i, 0)),
                  pl.BlockSpec((tt, PEER_SEL), lambda i, e: (i, 0)),
                  pl.BlockSpec((tt, D_MODEL), lambda i, e: (i, 0)),
                  pl.BlockSpec((tt, D_MODEL), lambda i, e: (i, 0)),
                  pl.BlockSpec((eb, D_MODEL), lambda i, e: (e, 0)),
                  pl.BlockSpec((eb, D_MODEL), lambda i, e: (e, 0)),
                  pl.BlockSpec((1, D_MODEL), lambda i, e: (0, 0)),
                  pl.BlockSpec((1, D_MODEL), lambda i, e: (0, 0))],
        out_specs=pl.BlockSpec((tt, D_MODEL), lambda i, e: (i, 0)),
        scratch_shapes=[pltpu.VMEM((PEER_N_KEYS * pitch, PEER_N_KEYS), F32),
                        pltpu.VMEM((tt, eb), BF16),
                        pltpu.VMEM((tt, D_MODEL), F32)],
        compiler_params=_cparams(("parallel", "arbitrary")),
        name="experts",
    )(ids, gates, h2b, h2, u_bf16, v_bf16, g, b)


def _rope_tables(n_slots, head_dim, lanes):
    inv = ROPE_THETA ** (-jnp.arange(0, head_dim, 2, dtype=F32) / head_dim)
    pos = jnp.arange(n_slots, dtype=jnp.int32) - PAD
    ang = pos.astype(F32)[:, None] * inv[None, :]
    cos, sin = jnp.cos(ang), jnp.sin(ang)
    reps = lanes // head_dim
    cos_t = jnp.tile(jnp.concatenate([cos, cos], axis=1), (1, reps))
    sin_t = jnp.tile(jnp.concatenate([-sin, sin], axis=1), (1, reps))
    return cos_t, sin_t


def _decay_tables():
    log_gamma = jnp.log(1.0 - 2.0 ** (-5.0 - jnp.arange(RET_HEADS, dtype=F32)))
    idx = jnp.arange(BLOCK, dtype=F32)
    rel = idx[:, None] - idx[None, :]
    inner = jnp.where(rel[None] >= 0,
                      jnp.exp(jnp.maximum(rel, 0.0)[None] * log_gamma[:, None, None]), 0.0)
    qdec = jnp.exp((idx[None, :] + 1.0) * log_gamma[:, None])[:, :, None]
    kdec = jnp.exp((BLOCK - 1.0 - idx[None, :]) * log_gamma[:, None])[:, :, None]
    chunk = jnp.exp(BLOCK * log_gamma)
    return inner, qdec, kdec, chunk


def _tile(n, pref):
    return pref if n % pref == 0 else n


def kernel(x, meta_tokens, ln_in_g, ln_in_b, w_in, att_sinks, w_out, ln_mix_g, ln_mix_b,
           peer_wq, peer_subkeys, peer_u, peer_v, ln_ffn_g, ln_ffn_b):
    batch, seq, d = x.shape
    assert d == D_MODEL and seq % BLOCK == 0 and w_in.shape[0] == DEPTH
    nbk = seq // BLOCK
    t = batch * seq
    row = lambda v: v.reshape(1, D_MODEL).astype(F32)

    x2d = x.reshape(t, D_MODEL)
    x_meta = jnp.concatenate([jnp.zeros((PAD, D_MODEL), x.dtype), meta_tokens.astype(x.dtype)], axis=0)
    w_in_b = w_in[0].astype(BF16)

    h, proj = _inproj(x2d, row(ln_in_g), row(ln_in_b), w_in_b, _tile(t, INPROJ_TM))
    _, proj_meta = _inproj(x_meta, row(ln_in_g), row(ln_in_b), w_in_b, BLOCK)

    n_slots = seq + BLOCK
    cos_a, sin_a = _rope_tables(n_slots, ATT_HEAD_DIM, 128)
    cos_r, sin_r = _rope_tables(n_slots, RET_HEAD_DIM, RET_HEAD_DIM)
    inner, qdec, kdec, chunk = _decay_tables()

    att = _attention(proj, proj_meta, att_sinks[0].astype(F32), cos_a, sin_a, batch, nbk)
    ret = _retention(proj, proj_meta, cos_r, sin_r, inner, qdec, kdec, chunk, batch, nbk)

    h2, h2b = _mix(att, ret, h, w_out[0].astype(BF16), row(ln_mix_g[0]), row(ln_mix_b[0]),
                   _tile(t, MIX_TM))

    ids, gates = _route(h2b, peer_wq[0].astype(BF16), peer_subkeys[0].astype(BF16),
                        _tile(t, ROUTE_TM))
    out = _experts(ids, gates, h2b, h2, peer_u[0].astype(BF16), peer_v[0].astype(BF16),
                   row(ln_ffn_g[0]), row(ln_ffn_b[0]), _tile(t, EXPERT_TT), EXPERT_EB)
    return out.reshape(batch, seq, D_MODEL)
```

```python
import functools

import jax
import jax.numpy as jnp
from jax import lax
from jax.experimental import pallas as pl
from jax.experimental.pallas import tpu as pltpu

F32 = jnp.float32
BF16 = jnp.bfloat16

D_MODEL = 2048
DEPTH = 1
N_META = 16
BLOCK = 128
PAD = BLOCK - N_META

ATT_WIDTH = 1024
ATT_HEAD_DIM = 64
ATT_Q_HEADS = 16
ATT_KV_HEADS = 2
ATT_GROUP = 8
ATT_KV_WIDTH = 128
ROPE_THETA = 10000.0

RET_WIDTH = 1024
RET_HEAD_DIM = 256
RET_HEADS = 4

IN_COLS = ATT_WIDTH + 2 * ATT_KV_WIDTH + 4 * RET_WIDTH
COL_RQ, COL_RK, COL_RV, COL_RG = 1, 2, 3, 4
COL_AK = (ATT_WIDTH + 4 * RET_WIDTH) // ATT_KV_WIDTH
COL_AV = COL_AK + 1

PEER_HEADS = 8
PEER_N_KEYS = 128
PEER_N_EXPERTS = PEER_N_KEYS * PEER_N_KEYS
PEER_HALF = 128
PEER_TOPK = 16
PEER_SEL = PEER_HEADS * PEER_TOPK

DEEPNORM_ALPHA = (2 * DEPTH) ** 0.25
LN_EPS = 1e-5
GN_EPS = 1e-5
NEG = -1e30

VMEM_LIMIT_BYTES = 56 * 1024 * 1024
EXPERT_VMEM_LIMIT_BYTES = 60000 * 1024

INPROJ_TM = 512
INPROJ_TN = 768
MIX_TM = 256
ROUTE_TM = 256
EXPERT_TT = 512
EXPERT_EB = 512
FFN_LN_TM = 512
GATE_PITCH_PAD = 8
GATE_BUILD_UNROLL = 8


def _layer_norm(x, g, b):
    mu = jnp.mean(x, axis=-1, keepdims=True)
    xc = x - mu
    var = jnp.mean(xc * xc, axis=-1, keepdims=True)
    return xc * lax.rsqrt(var + LN_EPS) * g + b


def _cparams(sem):
    return pltpu.CompilerParams(dimension_semantics=sem, vmem_limit_bytes=VMEM_LIMIT_BYTES)


def _inproj_kernel(x_ref, g_ref, b_ref, w_ref, h_ref, proj_ref):
    h = _layer_norm(x_ref[...], g_ref[...], b_ref[...])
    h_ref[...] = h
    hb = h.astype(BF16)
    for c in range(IN_COLS // INPROJ_TN):
        cols = slice(c * INPROJ_TN, (c + 1) * INPROJ_TN)
        proj_ref[:, cols] = jnp.dot(hb, w_ref[:, cols], preferred_element_type=F32).astype(BF16)


def _inproj(x2d, g, b, w_bf16, tm):
    t = x2d.shape[0]
    return pl.pallas_call(
        _inproj_kernel,
        out_shape=(jax.ShapeDtypeStruct((t, D_MODEL), F32),
                   jax.ShapeDtypeStruct((t, IN_COLS), BF16)),
        grid=(t // tm,),
        in_specs=[pl.BlockSpec((tm, D_MODEL), lambda i: (i, 0)),
                  pl.BlockSpec((1, D_MODEL), lambda i: (0, 0)),
                  pl.BlockSpec((1, D_MODEL), lambda i: (0, 0)),
                  pl.BlockSpec((D_MODEL, IN_COLS), lambda i: (0, 0), pipeline_mode=pl.Buffered(1))],
        out_specs=(pl.BlockSpec((tm, D_MODEL), lambda i: (i, 0)),
                   pl.BlockSpec((tm, IN_COLS), lambda i: (i, 0))),
        compiler_params=_cparams(("parallel",)),
        name="inproj",
    )(x2d, g, b, w_bf16)


def _att_rope(x, cos, sin_signed, lo_half):
    partner = jnp.where(lo_half, pltpu.roll(x, 96, 1), pltpu.roll(x, 32, 1))
    return x * cos + partner * sin_signed


def _att_kernel(sink_ref, q_ref, kc_ref, vc_ref, kp_ref, vp_ref, km_ref, vm_ref,
                cq_ref, sq_ref, cp_ref, sp_ref, o_ref):
    first = pl.program_id(1) == 0
    lane = lax.broadcasted_iota(jnp.int32, (BLOCK, 128), 1)
    lo_half = (lane % ATT_HEAD_DIM) < (ATT_HEAD_DIM // 2)
    cq, sq = cq_ref[...], sq_ref[...]

    k_prev = jnp.where(first, km_ref[...], kp_ref[...]).astype(F32)
    v_prev = jnp.where(first, vm_ref[...], vp_ref[...])
    k_prev = _att_rope(k_prev, cp_ref[...], sp_ref[...], lo_half)
    k_cur = _att_rope(kc_ref[...].astype(F32), cq, sq, lo_half)
    k_all = jnp.concatenate([k_prev, k_cur], axis=0).astype(BF16)
    v_all = jnp.concatenate([v_prev, vc_ref[...]], axis=0)

    qi = lax.broadcasted_iota(jnp.int32, (BLOCK, 2 * BLOCK), 0)
    kj = lax.broadcasted_iota(jnp.int32, (BLOCK, 2 * BLOCK), 1)
    prev_ok = (kj > qi) & (jnp.logical_not(first) | (kj >= PAD))
    mask = ((kj < BLOCK) & prev_ok) | ((kj >= BLOCK) & ((kj - BLOCK) <= qi))

    for c in range(ATT_Q_HEADS // 2):
        qc = _att_rope(q_ref[:, c * 128:(c + 1) * 128].astype(F32), cq, sq, lo_half).astype(BF16)
        for hh in range(2):
            h = 2 * c + hh
            g = h // ATT_GROUP
            qh = qc[:, hh * ATT_HEAD_DIM:(hh + 1) * ATT_HEAD_DIM]
            kg = k_all[:, g * ATT_HEAD_DIM:(g + 1) * ATT_HEAD_DIM]
            vg = v_all[:, g * ATT_HEAD_DIM:(g + 1) * ATT_HEAD_DIM]
            s = lax.dot_general(qh, kg, (((1,), (1,)), ((), ())),
                                preferred_element_type=F32) * (ATT_HEAD_DIM ** -0.5)
            s = jnp.where(mask, s, NEG)
            sink = sink_ref[h]
            m = jnp.maximum(jnp.max(s, axis=-1, keepdims=True), sink)
            p = jnp.exp(s - m)
            denom = jnp.sum(p, axis=-1, keepdims=True) + jnp.exp(sink - m)
            o = jnp.dot(p.astype(BF16), vg, preferred_element_type=F32) / denom
            o_ref[:, h * ATT_HEAD_DIM:(h + 1) * ATT_HEAD_DIM] = o.astype(BF16)


def _attention(proj, proj_meta, sinks, cos_a, sin_a, batch, nbk):
    t = proj.shape[0]

    def row(b, n):
        return b * nbk + n

    def prow(b, n):
        return b * nbk + jnp.maximum(n - 1, 0)

    blk = (BLOCK, ATT_KV_WIDTH)
    return pl.pallas_call(
        _att_kernel,
        out_shape=jax.ShapeDtypeStruct((t, ATT_WIDTH), BF16),
        grid=(batch, nbk),
        in_specs=[pl.BlockSpec(memory_space=pltpu.SMEM),
                  pl.BlockSpec((BLOCK, ATT_WIDTH), lambda b, n: (row(b, n), 0)),
                  pl.BlockSpec(blk, lambda b, n: (row(b, n), COL_AK)),
                  pl.BlockSpec(blk, lambda b, n: (row(b, n), COL_AV)),
                  pl.BlockSpec(blk, lambda b, n: (prow(b, n), COL_AK)),
                  pl.BlockSpec(blk, lambda b, n: (prow(b, n), COL_AV)),
                  pl.BlockSpec(blk, lambda b, n: (0, COL_AK)),
                  pl.BlockSpec(blk, lambda b, n: (0, COL_AV)),
                  pl.BlockSpec(blk, lambda b, n: (n + 1, 0)),
                  pl.BlockSpec(blk, lambda b, n: (n + 1, 0)),
                  pl.BlockSpec(blk, lambda b, n: (n, 0)),
                  pl.BlockSpec(blk, lambda b, n: (n, 0))],
        out_specs=pl.BlockSpec((BLOCK, ATT_WIDTH), lambda b, n: (row(b, n), 0)),
        compiler_params=_cparams(("parallel", "arbitrary")),
        name="attention",
    )(sinks, proj, proj, proj, proj, proj, proj_meta, proj_meta, cos_a, sin_a, cos_a, sin_a)


def _ret_rope(x, cos, sin_signed):
    partner = jnp.concatenate([x[:, RET_HEAD_DIM // 2:], x[:, :RET_HEAD_DIM // 2]], axis=1)
    return x * cos + partner * sin_signed


def _ret_kernel(cd_ref, q_ref, k_ref, v_ref, g_ref, km_ref, vm_ref, cq_ref, sq_ref, cm_ref, sm_ref,
                inner_ref, qd_ref, kd_ref, o_ref, state_ref):
    kscale = RET_HEAD_DIM ** -0.5
    cq, sq = cq_ref[...], sq_ref[...]

    @pl.when(pl.program_id(1) == 0)
    def _():
        valid = lax.broadcasted_iota(jnp.int32, (BLOCK, 1), 0) >= PAD
        cm, sm = cm_ref[...], sm_ref[...]
        for h in range(RET_HEADS):
            cols = slice(h * RET_HEAD_DIM, (h + 1) * RET_HEAD_DIM)
            km = _ret_rope(jnp.where(valid, km_ref[:, cols].astype(F32), 0.0), cm, sm) * kscale
            state_ref[h] = lax.dot_general((km * kd_ref[h]).astype(BF16), vm_ref[:, cols],
                                           (((0,), (0,)), ((), ())), preferred_element_type=F32)

    for h in range(RET_HEADS):
        cols = slice(h * RET_HEAD_DIM, (h + 1) * RET_HEAD_DIM)
        q = _ret_rope(q_ref[:, cols].astype(F32), cq, sq)
        k = _ret_rope(k_ref[:, cols].astype(F32), cq, sq) * kscale
        vb = v_ref[:, cols]
        state = state_ref[h]

        att = lax.dot_general(q.astype(BF16), k.astype(BF16), (((1,), (1,)), ((), ())),
                              preferred_element_type=F32) * inner_ref[h]
        o = (jnp.dot(att.astype(BF16), vb, preferred_element_type=F32)
             + jnp.dot((q * qd_ref[h]).astype(BF16), state.astype(BF16),
                       preferred_element_type=F32))
        state_ref[h] = state * cd_ref[h] + lax.dot_general(
            (k * kd_ref[h]).astype(BF16), vb, (((0,), (0,)), ((), ())), preferred_element_type=F32)

        mu = jnp.mean(o, axis=-1, keepdims=True)
        oc = o - mu
        var = jnp.mean(oc * oc, axis=-1, keepdims=True)
        gate = g_ref[:, cols].astype(F32)
        silu = gate * (1.0 / (1.0 + jnp.exp(-gate)))
        o_ref[:, cols] = (silu * (oc * lax.rsqrt(var + GN_EPS))).astype(BF16)


def _retention(proj, proj_meta, cos_r, sin_r, inner, qdec, kdec, chunk_decay, batch, nbk):
    t = proj.shape[0]
    blk = (BLOCK, RET_WIDTH)
    tab = (BLOCK, RET_HEAD_DIM)

    def row(b, n):
        return b * nbk + n

    return pl.pallas_call(
        _ret_kernel,
        out_shape=jax.ShapeDtypeStruct((t, RET_WIDTH), BF16),
        grid=(batch, nbk),
        in_specs=[pl.BlockSpec(memory_space=pltpu.SMEM),
                  pl.BlockSpec(blk, lambda b, n: (row(b, n), COL_RQ)),
                  pl.BlockSpec(blk, lambda b, n: (row(b, n), COL_RK)),
                  pl.BlockSpec(blk, lambda b, n: (row(b, n), COL_RV)),
                  pl.BlockSpec(blk, lambda b, n: (row(b, n), COL_RG)),
                  pl.BlockSpec(blk, lambda b, n: (0, COL_RK)),
                  pl.BlockSpec(blk, lambda b, n: (0, COL_RV)),
                  pl.BlockSpec(tab, lambda b, n: (n + 1, 0)),
                  pl.BlockSpec(tab, lambda b, n: (n + 1, 0)),
                  pl.BlockSpec(tab, lambda b, n: (0, 0)),
                  pl.BlockSpec(tab, lambda b, n: (0, 0)),
                  pl.BlockSpec((RET_HEADS, BLOCK, BLOCK), lambda b, n: (0, 0, 0)),
                  pl.BlockSpec((RET_HEADS, BLOCK, 1), lambda b, n: (0, 0, 0)),
                  pl.BlockSpec((RET_HEADS, BLOCK, 1), lambda b, n: (0, 0, 0))],
        out_specs=pl.BlockSpec(blk, lambda b, n: (row(b, n), 0)),
        scratch_shapes=[pltpu.VMEM((RET_HEADS, RET_HEAD_DIM, RET_HEAD_DIM), F32)],
        compiler_params=_cparams(("parallel", "arbitrary")),
        name="retention",
    )(chunk_decay, proj, proj, proj, proj, proj_meta, proj_meta, cos_r, sin_r, cos_r, sin_r,
      inner, qdec, kdec)


def _mix_kernel(att_ref, ret_ref, h_ref, w_ref, g_ref, b_ref, h2_ref, h2b_ref):
    mix = (jnp.dot(att_ref[...], w_ref[:ATT_WIDTH, :], preferred_element_type=F32)
           + jnp.dot(ret_ref[...], w_ref[ATT_WIDTH:, :], preferred_element_type=F32))
    h2 = _layer_norm(DEEPNORM_ALPHA * h_ref[...] + mix, g_ref[...], b_ref[...])
    h2_ref[...] = h2
    h2b_ref[...] = h2.astype(BF16)


def _mix(att, ret, h, w_bf16, g, b, tm):
    t = h.shape[0]
    return pl.pallas_call(
        _mix_kernel,
        out_shape=(jax.ShapeDtypeStruct((t, D_MODEL), F32),
                   jax.ShapeDtypeStruct((t, D_MODEL), BF16)),
        grid=(t // tm,),
        in_specs=[pl.BlockSpec((tm, ATT_WIDTH), lambda i: (i, 0)),
                  pl.BlockSpec((tm, RET_WIDTH), lambda i: (i, 0)),
                  pl.BlockSpec((tm, D_MODEL), lambda i: (i, 0)),
                  pl.BlockSpec((D_MODEL, D_MODEL), lambda i: (0, 0)),
                  pl.BlockSpec((1, D_MODEL), lambda i: (0, 0)),
                  pl.BlockSpec((1, D_MODEL), lambda i: (0, 0))],
        out_specs=(pl.BlockSpec((tm, D_MODEL), lambda i: (i, 0)),
                   pl.BlockSpec((tm, D_MODEL), lambda i: (i, 0))),
        compiler_params=_cparams(("parallel",)),
        name="mix",
    )(att, ret, h, w_bf16, g, b)


def _top16_rows(streams):
    big = jnp.iinfo(jnp.int32).max
    vals = [s[0] for s in streams]
    for k in range(PEER_TOPK):
        for j, (_, order, emit) in enumerate(streams):
            m = jnp.max(vals[j], axis=0, keepdims=True)
            idx = jnp.min(jnp.where(vals[j] == m, order, big), axis=0, keepdims=True)
            picked = order == idx
            emit(k, m, idx, picked)
            vals[j] = jnp.where(picked, -jnp.inf, vals[j])


def _candidate_groups(sv_ref, si_ref):
    sv0, sv1, si0, si1 = sv_ref.at[0], sv_ref.at[1], si_ref.at[0], si_ref.at[1]
    row8 = lax.broadcasted_iota(jnp.int32, (8, sv_ref.shape[-1]), 0)
    vals, flat, ids = [], [], []

    def add(v, f, i, keep=None):
        vals.append(v if keep is None else jnp.where(keep, v, -jnp.inf))
        flat.append(f)
        ids.append(i)

    for a in range(2):
        for b0 in range(0, PEER_TOPK // (a + 1), 8):
            add(sv0[a:a + 1] + sv1[b0:b0 + 8], a * PEER_TOPK + b0 + row8,
                si0[a:a + 1] * PEER_N_KEYS + si1[b0:b0 + 8])
    add(sv0[8:16] + sv1[0:1], (row8 + 8) * PEER_TOPK, si0[8:16] * PEER_N_KEYS + si1[0:1])
    for b in range(5):
        a_hi = PEER_TOPK // (b + 1) - 1
        add(sv0[0:8] + sv1[b:b + 1], row8 * PEER_TOPK + b, si0[0:8] * PEER_N_KEYS + si1[b:b + 1],
            keep=(row8 >= 2) & (row8 <= a_hi))
    return (jnp.concatenate(vals, axis=0), jnp.concatenate(flat, axis=0),
            jnp.concatenate(ids, axis=0))


def _route_kernel(hb_ref, wq_ref, sk_ref, ids_ref, gates_ref,
                  q_scr, sv_scr, si_scr, fv_scr, fi_scr, ids_t, gates_t, *, tm):
    q_scr[...] = jnp.dot(hb_ref[...], wq_ref[...], preferred_element_type=F32)
    key_rows = lax.broadcasted_iota(jnp.int32, (PEER_N_KEYS, 128), 0)

    def head_body(h, carry):
        base = pl.multiple_of(h * PEER_TOPK, PEER_TOPK)
        n_groups = tm // 128
        for lg in range(n_groups):
            streams = []
            for p in range(2):
                off = pl.multiple_of(h * (2 * PEER_HALF) + p * PEER_HALF, PEER_HALF)
                qhp = q_scr[lg * 128:(lg + 1) * 128, pl.ds(off, PEER_HALF)].astype(BF16)
                s = lax.dot_general(sk_ref[h, p], qhp, (((1,), (1,)), ((), ())),
                                    preferred_element_type=F32)

                def emit1(k, m, idx, picked, lg=lg, p=p):
                    sv_scr[lg, p, k:k + 1, :] = m
                    si_scr[lg, p, k:k + 1, :] = idx

                streams.append((s, key_rows, emit1))
            _top16_rows(streams)

        streams = []
        for lg in range(n_groups):
            cand, flat, idc = _candidate_groups(sv_scr.at[lg], si_scr.at[lg])

            def emit2(k, m, idx, picked, lg=lg, idc=idc):
                fv_scr[lg, k:k + 1, :] = m
                fi_scr[lg, k:k + 1, :] = jnp.max(jnp.where(picked, idc, -1), axis=0, keepdims=True)

            streams.append((cand, flat, emit2))
        _top16_rows(streams)

        for lg in range(n_groups):
            cols = slice(lg * 128, (lg + 1) * 128)
            ids_t[pl.ds(base, PEER_TOPK), cols] = fi_scr[lg]
            fv = fv_scr[lg]
            e = jnp.exp(fv - fv[0:1, :])
            gates_t[pl.ds(base, PEER_TOPK), cols] = e / jnp.sum(e, axis=0, keepdims=True)
        return carry

    lax.fori_loop(0, PEER_HEADS, head_body, 0)
    ids_ref[...] = ids_t[...].T
    gates_ref[...] = gates_t[...].T


def _route(h2b, wq_bf16, sk_bf16, tm):
    t = h2b.shape[0]
    return pl.pallas_call(
        functools.partial(_route_kernel, tm=tm),
        out_shape=(jax.ShapeDtypeStruct((t, PEER_SEL), jnp.int32),
                   jax.ShapeDtypeStruct((t, PEER_SEL), F32)),
        grid=(t // tm,),
        in_specs=[pl.BlockSpec((tm, D_MODEL), lambda i: (i, 0)),
                  pl.BlockSpec((D_MODEL, PEER_HEADS * 2 * PEER_HALF), lambda i: (0, 0)),
                  pl.BlockSpec((PEER_HEADS, 2, PEER_N_KEYS, PEER_HALF), lambda i: (0, 0, 0, 0))],
        out_specs=(pl.BlockSpec((tm, PEER_SEL), lambda i: (i, 0)),
                   pl.BlockSpec((tm, PEER_SEL), lambda i: (i, 0))),
        scratch_shapes=[pltpu.VMEM((tm, PEER_HEADS * 2 * PEER_HALF), F32),
                        pltpu.VMEM((tm // 128, 2, PEER_TOPK, 128), F32),
                        pltpu.VMEM((tm // 128, 2, PEER_TOPK, 128), jnp.int32),
                        pltpu.VMEM((tm // 128, PEER_TOPK, 128), F32),
                        pltpu.VMEM((tm // 128, PEER_TOPK, 128), jnp.int32),
                        pltpu.VMEM((PEER_SEL, tm), jnp.int32),
                        pltpu.VMEM((PEER_SEL, tm), F32)],
        compiler_params=_cparams(("parallel",)),
        name="route",
    )(h2b, wq_bf16, sk_bf16)


def _gelu_exact(x):
    return 0.5 * x * (1.0 + lax.erf(x * (2.0 ** -0.5)))


def _expert_kernel(ids_ref, gates_ref, hb_ref, u_ref, v_ref, o_ref, wg_scr, gact_scr, *, tt, eb):
    e = pl.program_id(1)
    pitch = tt + GATE_PITCH_PAD
    n_sub = eb // PEER_N_KEYS

    @pl.when(e == 0)
    def _():
        o_ref[...] = jnp.zeros_like(o_ref)
        sub = lax.broadcasted_iota(jnp.int32, (PEER_N_KEYS, PEER_SEL), 0)

        def token_body(t, carry):
            ids = ids_ref[pl.ds(t, 1), :]
            gates = gates_ref[pl.ds(t, 1), :]
            left = jnp.where(sub == (ids // PEER_N_KEYS), gates, 0.0).astype(BF16)
            right = jnp.where(sub == (ids % PEER_N_KEYS), 1.0, 0.0).astype(BF16)
            w_t = lax.dot_general(left, right, (((1,), (1,)), ((), ())),
                                  preferred_element_type=F32)
            wg_scr[pl.ds(t, PEER_N_KEYS, stride=pitch), :] = w_t
            return carry

        lax.fori_loop(0, tt, token_body, 0, unroll=GATE_BUILD_UNROLL)

    a = lax.dot_general(hb_ref[...], u_ref[...], (((1,), (1,)), ((), ())),
                        preferred_element_type=F32)
    for c in range(n_sub):
        start = pl.multiple_of((e * n_sub + c) * pitch, 8)
        wg = wg_scr[pl.ds(start, tt), :]
        ac = a[:, c * PEER_N_KEYS:(c + 1) * PEER_N_KEYS]
        gact_scr[:, c * PEER_N_KEYS:(c + 1) * PEER_N_KEYS] = (_gelu_exact(ac) * wg).astype(BF16)
    o_ref[...] += jnp.dot(gact_scr[...], v_ref[...], preferred_element_type=F32)


def _experts(ids, gates, h2b, u_bf16, v_bf16, tt, eb):
    t = h2b.shape[0]
    pitch = tt + GATE_PITCH_PAD
    once = dict(pipeline_mode=pl.Buffered(1))
    return pl.pallas_call(
        functools.partial(_expert_kernel, tt=tt, eb=eb),
        out_shape=jax.ShapeDtypeStruct((t, D_MODEL), F32),
        grid=(t // tt, PEER_N_EXPERTS // eb),
        in_specs=[pl.BlockSpec((tt, PEER_SEL), lambda i, e: (i, 0), **once),
                  pl.BlockSpec((tt, PEER_SEL), lambda i, e: (i, 0), **once),
                  pl.BlockSpec((tt, D_MODEL), lambda i, e: (i, 0), **once),
                  pl.BlockSpec((eb, D_MODEL), lambda i, e: (e, 0)),
                  pl.BlockSpec((eb, D_MODEL), lambda i, e: (e, 0))],
        out_specs=pl.BlockSpec((tt, D_MODEL), lambda i, e: (i, 0)),
        scratch_shapes=[pltpu.VMEM((PEER_N_KEYS * pitch, PEER_N_KEYS), F32),
                        pltpu.VMEM((tt, eb), BF16)],
        compiler_params=pltpu.CompilerParams(dimension_semantics=("parallel", "arbitrary"),
                                             vmem_limit_bytes=EXPERT_VMEM_LIMIT_BYTES),
        name="experts",
    )(ids, gates, h2b, u_bf16, v_bf16)


def _ffn_ln_kernel(h2_ref, ffn_ref, g_ref, b_ref, o_ref):
    o_ref[...] = _layer_norm(DEEPNORM_ALPHA * h2_ref[...] + ffn_ref[...], g_ref[...], b_ref[...])


def _ffn_ln(h2, ffn, g, b, tm):
    t = h2.shape[0]
    return pl.pallas_call(
        _ffn_ln_kernel,
        out_shape=jax.ShapeDtypeStruct((t, D_MODEL), F32),
        grid=(t // tm,),
        in_specs=[pl.BlockSpec((tm, D_MODEL), lambda i: (i, 0)),
                  pl.BlockSpec((tm, D_MODEL), lambda i: (i, 0)),
                  pl.BlockSpec((1, D_MODEL), lambda i: (0, 0)),
                  pl.BlockSpec((1, D_MODEL), lambda i: (0, 0))],
        out_specs=pl.BlockSpec((tm, D_MODEL), lambda i: (i, 0)),
        compiler_params=_cparams(("parallel",)),
        name="ffn_ln",
    )(h2, ffn, g, b)


def _rope_tables(n_slots, head_dim, lanes):
    inv = ROPE_THETA ** (-jnp.arange(0, head_dim, 2, dtype=F32) / head_dim)
    pos = jnp.arange(n_slots, dtype=jnp.int32) - PAD
    ang = pos.astype(F32)[:, None] * inv[None, :]
    cos, sin = jnp.cos(ang), jnp.sin(ang)
    reps = lanes // head_dim
    cos_t = jnp.tile(jnp.concatenate([cos, cos], axis=1), (1, reps))
    sin_t = jnp.tile(jnp.concatenate([-sin, sin], axis=1), (1, reps))
    return cos_t, sin_t


def _decay_tables():
    log_gamma = jnp.log(1.0 - 2.0 ** (-5.0 - jnp.arange(RET_HEADS, dtype=F32)))
    idx = jnp.arange(BLOCK, dtype=F32)
    rel = idx[:, None] - idx[None, :]
    inner = jnp.where(rel[None] >= 0,
                      jnp.exp(jnp.maximum(rel, 0.0)[None] * log_gamma[:, None, None]), 0.0)
    qdec = jnp.exp((idx[None, :] + 1.0) * log_gamma[:, None])[:, :, None]
    kdec = jnp.exp((BLOCK - 1.0 - idx[None, :]) * log_gamma[:, None])[:, :, None]
    chunk = jnp.exp(BLOCK * log_gamma)
    return inner, qdec, kdec, chunk


def _tile(n, pref):
    return pref if n % pref == 0 else n


def kernel(x, meta_tokens, ln_in_g, ln_in_b, w_in, att_sinks, w_out, ln_mix_g, ln_mix_b,
           peer_wq, peer_subkeys, peer_u, peer_v, ln_ffn_g, ln_ffn_b):
    batch, seq, d = x.shape
    assert d == D_MODEL and seq % BLOCK == 0 and w_in.shape[0] == DEPTH
    nbk = seq // BLOCK
    t = batch * seq
    row = lambda v: v.reshape(1, D_MODEL).astype(F32)

    x2d = x.reshape(t, D_MODEL)
    x_meta = jnp.concatenate([jnp.zeros((PAD, D_MODEL), x.dtype), meta_tokens.astype(x.dtype)], axis=0)
    kv0, kv1 = ATT_WIDTH, ATT_WIDTH + 2 * ATT_KV_WIDTH
    w_in_b = jnp.concatenate([w_in[0, :, :kv0], w_in[0, :, kv1:], w_in[0, :, kv0:kv1]],
                             axis=1).astype(BF16)

    h, proj = _inproj(x2d, row(ln_in_g), row(ln_in_b), w_in_b, _tile(t, INPROJ_TM))
    _, proj_meta = _inproj(x_meta, row(ln_in_g), row(ln_in_b), w_in_b, BLOCK)

    n_slots = seq + BLOCK
    cos_a, sin_a = _rope_tables(n_slots, ATT_HEAD_DIM, 128)
    cos_r, sin_r = _rope_tables(n_slots, RET_HEAD_DIM, RET_HEAD_DIM)
    inner, qdec, kdec, chunk = _decay_tables()

    att = _attention(proj, proj_meta, att_sinks[0].astype(F32), cos_a, sin_a, batch, nbk)
    ret = _retention(proj, proj_meta, cos_r, sin_r, inner, qdec, kdec, chunk, batch, nbk)

    h2, h2b = _mix(att, ret, h, w_out[0].astype(BF16), row(ln_mix_g[0]), row(ln_mix_b[0]),
                   _tile(t, MIX_TM))

    ids, gates = _route(h2b, peer_wq[0].astype(BF16), peer_subkeys[0].astype(BF16),
                        _tile(t, ROUTE_TM))
    ffn = _experts(ids, gates, h2b, peer_u[0].astype(BF16), peer_v[0].astype(BF16),
                   _tile(t, EXPERT_TT), EXPERT_EB)
    out = _ffn_ln(h2, ffn, row(ln_ffn_g[0]), row(ln_ffn_b[0]), _tile(t, FFN_LN_TM))
    return out.reshape(batch, seq, D_MODEL)
```

```python
import functools

import jax
import jax.numpy as jnp
from jax import lax
from jax.experimental import pallas as pl
from jax.experimental.pallas import tpu as pltpu

F32 = jnp.float32
BF16 = jnp.bfloat16

D_MODEL = 2048
DEPTH = 1
N_META = 16
BLOCK = 128
PAD = BLOCK - N_META

ATT_WIDTH = 1024
ATT_HEAD_DIM = 64
ATT_Q_HEADS = 16
ATT_KV_HEADS = 2
ATT_GROUP = 8
ATT_KV_WIDTH = 128
ROPE_THETA = 10000.0

RET_WIDTH = 1024
RET_HEAD_DIM = 256
RET_HEADS = 4

IN_COLS = ATT_WIDTH + 2 * ATT_KV_WIDTH + 4 * RET_WIDTH
COL_RQ, COL_RK, COL_RV, COL_RG = 1, 2, 3, 4
COL_AK = (ATT_WIDTH + 4 * RET_WIDTH) // ATT_KV_WIDTH
COL_AV = COL_AK + 1

PEER_HEADS = 8
PEER_N_KEYS = 128
PEER_N_EXPERTS = PEER_N_KEYS * PEER_N_KEYS
PEER_HALF = 128
PEER_TOPK = 16
PEER_SEL = PEER_HEADS * PEER_TOPK

DEEPNORM_ALPHA = (2 * DEPTH) ** 0.25
LN_EPS = 1e-5
GN_EPS = 1e-5
NEG = -1e30

VMEM_LIMIT_BYTES = 56 * 1024 * 1024
EXPERT_VMEM_LIMIT_BYTES = 60000 * 1024

INPROJ_TM = 512
INPROJ_TN = 768
MIX_TM = 512
EXPERT_TT = 512
EXPERT_UNITS_PER_STEP = 2
GATE_PITCH_PAD = 8
GATE_BUILD_UNROLL = 32


def _layer_norm(x, g, b):
    mu = jnp.mean(x, axis=-1, keepdims=True)
    xc = x - mu
    var = jnp.mean(xc * xc, axis=-1, keepdims=True)
    return xc * lax.rsqrt(var + LN_EPS) * g + b


def _cparams(sem):
    return pltpu.CompilerParams(dimension_semantics=sem, vmem_limit_bytes=VMEM_LIMIT_BYTES)


def _inproj_kernel(x_ref, g_ref, b_ref, w_ref, h_ref, proj_ref):
    h = _layer_norm(x_ref[...], g_ref[...], b_ref[...])
    h_ref[...] = h
    hb = h.astype(BF16)
    for c in range(IN_COLS // INPROJ_TN):
        cols = slice(c * INPROJ_TN, (c + 1) * INPROJ_TN)
        proj_ref[:, cols] = jnp.dot(hb, w_ref[:, cols], preferred_element_type=F32).astype(BF16)


def _inproj(x2d, g, b, w_bf16, tm):
    t = x2d.shape[0]
    return pl.pallas_call(
        _inproj_kernel,
        out_shape=(jax.ShapeDtypeStruct((t, D_MODEL), F32),
                   jax.ShapeDtypeStruct((t, IN_COLS), BF16)),
        grid=(t // tm,),
        in_specs=[pl.BlockSpec((tm, D_MODEL), lambda i: (i, 0)),
                  pl.BlockSpec((1, D_MODEL), lambda i: (0, 0)),
                  pl.BlockSpec((1, D_MODEL), lambda i: (0, 0)),
                  pl.BlockSpec((D_MODEL, IN_COLS), lambda i: (0, 0), pipeline_mode=pl.Buffered(1))],
        out_specs=(pl.BlockSpec((tm, D_MODEL), lambda i: (i, 0)),
                   pl.BlockSpec((tm, IN_COLS), lambda i: (i, 0))),
        compiler_params=_cparams(("parallel",)),
        name="inproj",
    )(x2d, g, b, w_bf16)


def _att_rope(x, cos, sin_signed, lo_half):
    partner = jnp.where(lo_half, pltpu.roll(x, 96, 1), pltpu.roll(x, 32, 1))
    return x * cos + partner * sin_signed


def _att_kernel(sink_ref, q_ref, kc_ref, vc_ref, kp_ref, vp_ref, km_ref, vm_ref,
                cq_ref, sq_ref, cp_ref, sp_ref, o_ref):
    first = pl.program_id(1) == 0
    lane = lax.broadcasted_iota(jnp.int32, (BLOCK, 128), 1)
    lo_half = (lane % ATT_HEAD_DIM) < (ATT_HEAD_DIM // 2)
    cq, sq = cq_ref[...], sq_ref[...]

    k_prev = jnp.where(first, km_ref[...], kp_ref[...]).astype(F32)
    v_prev = jnp.where(first, vm_ref[...], vp_ref[...])
    k_prev = _att_rope(k_prev, cp_ref[...], sp_ref[...], lo_half)
    k_cur = _att_rope(kc_ref[...].astype(F32), cq, sq, lo_half)
    k_all = jnp.concatenate([k_prev, k_cur], axis=0).astype(BF16)
    v_all = jnp.concatenate([v_prev, vc_ref[...]], axis=0)

    qi = lax.broadcasted_iota(jnp.int32, (BLOCK, 2 * BLOCK), 0)
    kj = lax.broadcasted_iota(jnp.int32, (BLOCK, 2 * BLOCK), 1)
    prev_ok = (kj > qi) & (jnp.logical_not(first) | (kj >= PAD))
    mask = ((kj < BLOCK) & prev_ok) | ((kj >= BLOCK) & ((kj - BLOCK) <= qi))

    ones = jnp.ones((2 * BLOCK, ATT_HEAD_DIM), BF16)
    for g in range(ATT_KV_HEADS):
        heads = range(g * ATT_GROUP, (g + 1) * ATT_GROUP)
        q_parts = []
        for c in range(g * ATT_GROUP // 2, (g + 1) * ATT_GROUP // 2):
            qc = _att_rope(q_ref[:, c * 128:(c + 1) * 128].astype(F32), cq, sq, lo_half).astype(BF16)
            q_parts += [qc[:, :ATT_HEAD_DIM], qc[:, ATT_HEAD_DIM:]]
        kg = k_all[:, g * ATT_HEAD_DIM:(g + 1) * ATT_HEAD_DIM]
        vg = v_all[:, g * ATT_HEAD_DIM:(g + 1) * ATT_HEAD_DIM]
        s_all = lax.dot_general(jnp.concatenate(q_parts, axis=0), kg, (((1,), (1,)), ((), ())),
                                preferred_element_type=F32)
        p_parts, sink_terms = [], []
        for j, h in enumerate(heads):
            s = s_all[j * BLOCK:(j + 1) * BLOCK] * (ATT_HEAD_DIM ** -0.5)
            s = jnp.where(mask, s, NEG)
            sink = sink_ref[h]
            m = jnp.maximum(jnp.max(s, axis=-1, keepdims=True), sink)
            p_parts.append(jnp.exp(s - m).astype(BF16))
            sink_terms.append(jnp.exp(sink - m))
        p_all = jnp.concatenate(p_parts, axis=0)
        o_all = jnp.dot(p_all, vg, preferred_element_type=F32)
        sum_all = jnp.dot(p_all, ones, preferred_element_type=F32)
        for j, h in enumerate(heads):
            rows = slice(j * BLOCK, (j + 1) * BLOCK)
            o = o_all[rows] / (sum_all[rows] + sink_terms[j])
            o_ref[:, h * ATT_HEAD_DIM:(h + 1) * ATT_HEAD_DIM] = o.astype(BF16)


def _attention(proj, proj_meta, sinks, cos_a, sin_a, batch, nbk):
    t = proj.shape[0]

    def row(b, n):
        return b * nbk + n

    def prow(b, n):
        return b * nbk + jnp.maximum(n - 1, 0)

    blk = (BLOCK, ATT_KV_WIDTH)
    return pl.pallas_call(
        _att_kernel,
        out_shape=jax.ShapeDtypeStruct((t, ATT_WIDTH), BF16),
        grid=(batch, nbk),
        in_specs=[pl.BlockSpec(memory_space=pltpu.SMEM),
                  pl.BlockSpec((BLOCK, ATT_WIDTH), lambda b, n: (row(b, n), 0)),
                  pl.BlockSpec(blk, lambda b, n: (row(b, n), COL_AK)),
                  pl.BlockSpec(blk, lambda b, n: (row(b, n), COL_AV)),
                  pl.BlockSpec(blk, lambda b, n: (prow(b, n), COL_AK)),
                  pl.BlockSpec(blk, lambda b, n: (prow(b, n), COL_AV)),
                  pl.BlockSpec(blk, lambda b, n: (0, COL_AK)),
                  pl.BlockSpec(blk, lambda b, n: (0, COL_AV)),
                  pl.BlockSpec(blk, lambda b, n: (n + 1, 0)),
                  pl.BlockSpec(blk, lambda b, n: (n + 1, 0)),
                  pl.BlockSpec(blk, lambda b, n: (n, 0)),
                  pl.BlockSpec(blk, lambda b, n: (n, 0))],
        out_specs=pl.BlockSpec((BLOCK, ATT_WIDTH), lambda b, n: (row(b, n), 0)),
        compiler_params=_cparams(("parallel", "arbitrary")),
        name="attention",
    )(sinks, proj, proj, proj, proj, proj, proj_meta, proj_meta, cos_a, sin_a, cos_a, sin_a)


def _ret_rope(x, cos, sin_signed):
    partner = jnp.concatenate([x[:, RET_HEAD_DIM // 2:], x[:, :RET_HEAD_DIM // 2]], axis=1)
    return x * cos + partner * sin_signed


def _ret_kernel(cd_ref, q_ref, k_ref, v_ref, g_ref, km_ref, vm_ref, cq_ref, sq_ref, cm_ref, sm_ref,
                inner_ref, qd_ref, kd_ref, o_ref, state_ref):
    kscale = RET_HEAD_DIM ** -0.5
    cq, sq = cq_ref[...], sq_ref[...]

    @pl.when(pl.program_id(1) == 0)
    def _():
        valid = lax.broadcasted_iota(jnp.int32, (BLOCK, 1), 0) >= PAD
        cm, sm = cm_ref[...], sm_ref[...]
        for h in range(RET_HEADS):
            cols = slice(h * RET_HEAD_DIM, (h + 1) * RET_HEAD_DIM)
            km = _ret_rope(jnp.where(valid, km_ref[:, cols].astype(F32), 0.0), cm, sm) * kscale
            state_ref[h] = lax.dot_general((km * kd_ref[h]).astype(BF16), vm_ref[:, cols],
                                           (((0,), (0,)), ((), ())), preferred_element_type=F32)

    for h in range(RET_HEADS):
        cols = slice(h * RET_HEAD_DIM, (h + 1) * RET_HEAD_DIM)
        q = _ret_rope(q_ref[:, cols].astype(F32), cq, sq)
        k = _ret_rope(k_ref[:, cols].astype(F32), cq, sq) * kscale
        vb = v_ref[:, cols]
        state = state_ref[h]

        att = lax.dot_general(q.astype(BF16), k.astype(BF16), (((1,), (1,)), ((), ())),
                              preferred_element_type=F32) * inner_ref[h]
        o = (jnp.dot(att.astype(BF16), vb, preferred_element_type=F32)
             + jnp.dot((q * qd_ref[h]).astype(BF16), state.astype(BF16),
                       preferred_element_type=F32))
        state_ref[h] = state * cd_ref[h] + lax.dot_general(
            (k * kd_ref[h]).astype(BF16), vb, (((0,), (0,)), ((), ())), preferred_element_type=F32)

        mu = jnp.mean(o, axis=-1, keepdims=True)
        oc = o - mu
        var = jnp.mean(oc * oc, axis=-1, keepdims=True)
        gate = g_ref[:, cols].astype(F32)
        silu = gate * (1.0 / (1.0 + jnp.exp(-gate)))
        o_ref[:, cols] = (silu * (oc * lax.rsqrt(var + GN_EPS))).astype(BF16)


def _retention(proj, proj_meta, cos_r, sin_r, inner, qdec, kdec, chunk_decay, batch, nbk):
    t = proj.shape[0]
    blk = (BLOCK, RET_WIDTH)
    tab = (BLOCK, RET_HEAD_DIM)

    def row(b, n):
        return b * nbk + n

    return pl.pallas_call(
        _ret_kernel,
        out_shape=jax.ShapeDtypeStruct((t, RET_WIDTH), BF16),
        grid=(batch, nbk),
        in_specs=[pl.BlockSpec(memory_space=pltpu.SMEM),
                  pl.BlockSpec(blk, lambda b, n: (row(b, n), COL_RQ)),
                  pl.BlockSpec(blk, lambda b, n: (row(b, n), COL_RK)),
                  pl.BlockSpec(blk, lambda b, n: (row(b, n), COL_RV)),
                  pl.BlockSpec(blk, lambda b, n: (row(b, n), COL_RG)),
                  pl.BlockSpec(blk, lambda b, n: (0, COL_RK)),
                  pl.BlockSpec(blk, lambda b, n: (0, COL_RV)),
                  pl.BlockSpec(tab, lambda b, n: (n + 1, 0)),
                  pl.BlockSpec(tab, lambda b, n: (n + 1, 0)),
                  pl.BlockSpec(tab, lambda b, n: (0, 0)),
                  pl.BlockSpec(tab, lambda b, n: (0, 0)),
                  pl.BlockSpec((RET_HEADS, BLOCK, BLOCK), lambda b, n: (0, 0, 0)),
                  pl.BlockSpec((RET_HEADS, BLOCK, 1), lambda b, n: (0, 0, 0)),
                  pl.BlockSpec((RET_HEADS, BLOCK, 1), lambda b, n: (0, 0, 0))],
        out_specs=pl.BlockSpec(blk, lambda b, n: (row(b, n), 0)),
        scratch_shapes=[pltpu.VMEM((RET_HEADS, RET_HEAD_DIM, RET_HEAD_DIM), F32)],
        compiler_params=_cparams(("parallel", "arbitrary")),
        name="retention",
    )(chunk_decay, proj, proj, proj, proj, proj_meta, proj_meta, cos_r, sin_r, cos_r, sin_r,
      inner, qdec, kdec)


def _mix_kernel(att_ref, ret_ref, h_ref, w_ref, wq_ref, g_ref, b_ref, h2_ref, h2b_ref, q_ref):
    mix = (jnp.dot(att_ref[...], w_ref[:ATT_WIDTH, :], preferred_element_type=F32)
           + jnp.dot(ret_ref[...], w_ref[ATT_WIDTH:, :], preferred_element_type=F32))
    h2 = _layer_norm(DEEPNORM_ALPHA * h_ref[...] + mix, g_ref[...], b_ref[...])
    h2_ref[...] = h2
    h2b = h2.astype(BF16)
    h2b_ref[...] = h2b
    q_ref[...] = jnp.dot(h2b, wq_ref[...], preferred_element_type=F32).astype(BF16)


def _mix(att, ret, h, w_bf16, wq_bf16, g, b, tm):
    t = h.shape[0]
    qw = PEER_HEADS * 2 * PEER_HALF
    once = dict(pipeline_mode=pl.Buffered(1))
    return pl.pallas_call(
        _mix_kernel,
        out_shape=(jax.ShapeDtypeStruct((t, D_MODEL), F32),
                   jax.ShapeDtypeStruct((t, D_MODEL), BF16),
                   jax.ShapeDtypeStruct((t, qw), BF16)),
        grid=(t // tm,),
        in_specs=[pl.BlockSpec((tm, ATT_WIDTH), lambda i: (i, 0)),
                  pl.BlockSpec((tm, RET_WIDTH), lambda i: (i, 0)),
                  pl.BlockSpec((tm, D_MODEL), lambda i: (i, 0)),
                  pl.BlockSpec((D_MODEL, D_MODEL), lambda i: (0, 0), **once),
                  pl.BlockSpec((D_MODEL, qw), lambda i: (0, 0), **once),
                  pl.BlockSpec((1, D_MODEL), lambda i: (0, 0)),
                  pl.BlockSpec((1, D_MODEL), lambda i: (0, 0))],
        out_specs=(pl.BlockSpec((tm, D_MODEL), lambda i: (i, 0)),
                   pl.BlockSpec((tm, D_MODEL), lambda i: (i, 0)),
                   pl.BlockSpec((tm, qw), lambda i: (i, 0))),
        compiler_params=_cparams(("parallel",)),
        name="mix",
    )(att, ret, h, w_bf16, wq_bf16, g, b)


def _top16_steps(load, emit):
    big = jnp.iinfo(jnp.int32).max
    st = {}

    def step(k):
        if k == 0:
            st["vals"], st["order"] = load()
        vals, order = st["vals"], st["order"]
        m = jnp.max(vals, axis=0, keepdims=True)
        idx = jnp.min(jnp.where(vals == m, order, big), axis=0, keepdims=True)
        picked = order == idx
        emit(k, m, idx, picked)
        st["vals"] = jnp.where(picked, -jnp.inf, vals)
        return m

    return [functools.partial(step, k) for k in range(PEER_TOPK)]


def _top16_key_steps(load, emit):
    big = jnp.iinfo(jnp.int32).max
    n_groups = PEER_N_KEYS // 8
    st = {}

    def step(k):
        if k == 0:
            tile = load()
            st["vals"] = [tile[8 * v:8 * (v + 1)] for v in range(n_groups)]
        vals = st["vals"]
        sub = lax.broadcasted_iota(jnp.int32, vals[0].shape, 0)
        m = jnp.max(functools.reduce(jnp.maximum, vals), axis=0, keepdims=True)
        cands = [jnp.where(vals[v] == m, sub + 8 * v, big) for v in range(n_groups)]
        idx = jnp.min(functools.reduce(jnp.minimum, cands), axis=0, keepdims=True)
        emit(k, m, idx, None)
        st["vals"] = [jnp.where(sub == idx - 8 * v, -jnp.inf, vals[v]) for v in range(n_groups)]
        return m

    return [functools.partial(step, k) for k in range(PEER_TOPK)]


def _candidate_groups(sv_ref, si_ref):
    sv0, sv1, si0, si1 = sv_ref.at[0], sv_ref.at[1], si_ref.at[0], si_ref.at[1]
    row8 = lax.broadcasted_iota(jnp.int32, (8, sv_ref.shape[-1]), 0)
    vals, flat, ids = [], [], []

    def add(v, f, i, keep=None):
        vals.append(v if keep is None else jnp.where(keep, v, -jnp.inf))
        flat.append(f)
        ids.append(i)

    for a in range(2):
        for b0 in range(0, PEER_TOPK // (a + 1), 8):
            add(sv0[a:a + 1] + sv1[b0:b0 + 8], a * PEER_TOPK + b0 + row8,
                si0[a:a + 1] * PEER_N_KEYS + si1[b0:b0 + 8])
    add(sv0[8:16] + sv1[0:1], (row8 + 8) * PEER_TOPK, si0[8:16] * PEER_N_KEYS + si1[0:1])
    for b in range(5):
        a_hi = PEER_TOPK // (b + 1) - 1
        add(sv0[0:8] + sv1[b:b + 1], row8 * PEER_TOPK + b, si0[0:8] * PEER_N_KEYS + si1[b:b + 1],
            keep=(row8 >= 2) & (row8 <= a_hi))
    return (jnp.concatenate(vals, axis=0), jnp.concatenate(flat, axis=0),
            jnp.concatenate(ids, axis=0))


def _route_scores(unit, n_groups, q_ref, sk_ref, s_ref):
    h = unit // n_groups
    tok = pl.multiple_of((unit % n_groups) * 128, 128)
    for p in range(2):
        off = pl.multiple_of(h * (2 * PEER_HALF) + p * PEER_HALF, PEER_HALF)
        qhp = q_ref[pl.ds(tok, 128), pl.ds(off, PEER_HALF)]
        s_ref[p] = lax.dot_general(sk_ref[h, p], qhp, (((1,), (1,)), ((), ())),
                                   preferred_element_type=F32)


def _exact_zero(x):
    bits = pltpu.bitcast(x, jnp.uint32)
    return lax.shift_right_logical(lax.shift_right_logical(bits, jnp.uint32(16)),
                                   jnp.uint32(16)).astype(F32)


def _route_select_steps(unit, n_groups, s_ref, sv_scr, si_scr, fv_scr, fi_scr, ids_t, gates_t,
                        after=None):
    h = unit // n_groups
    tok = pl.multiple_of((unit % n_groups) * 128, 128)
    steps = []
    for p in range(2):
        def load1(p=p):
            if after is None:
                return s_ref[p]
            return s_ref[p] + jnp.tile(_exact_zero(after), (PEER_N_KEYS // 8, 1))

        def emit1(k, m, idx, picked, p=p):
            sv_scr[p, k:k + 1, :] = m
            si_scr[p, k:k + 1, :] = idx

        steps += _top16_key_steps(load1, emit1)

    st = {}

    def load2():
        cand, flat, st["idc"] = _candidate_groups(sv_scr, si_scr)
        return cand, flat

    def emit2(k, m, idx, picked):
        fv_scr[k:k + 1, :] = m
        fi_scr[k:k + 1, :] = jnp.max(jnp.where(picked, st["idc"], -1), axis=0, keepdims=True)

    steps += _top16_steps(load2, emit2)

    def finish():
        base = pl.multiple_of(h * PEER_TOPK, PEER_TOPK)
        ids_t[pl.ds(base, PEER_TOPK), pl.ds(tok, 128)] = fi_scr[...]
        fv = fv_scr[...]
        ex = jnp.exp(fv - fv[0:1, :])
        gates_t[pl.ds(base, PEER_TOPK), pl.ds(tok, 128)] = ex / jnp.sum(ex, axis=0, keepdims=True)

    return steps + [finish]


def _gelu_exact_x2(x):
    return x * (1.0 + lax.erf(x * (2.0 ** -0.5)))


def _expert_kernel(q_ref, sk_ref, hb_ref, h2_ref, u_ref, v_ref, g_ref, b_ref, o_ref,
                   wg_scr, gact_scr, ids_t, gates_t, ids_rows, gates_rows,
                   s_scr, sv_scr, si_scr, fv_scr, fi_scr, *, tt, eb, ups):
    i = pl.program_id(0)
    e = pl.program_id(1)
    half = tt // 2
    pitch = half + GATE_PITCH_PAD
    n_sub = eb // PEER_N_KEYS
    n_groups = tt // 128

    @pl.when(e == 0)
    def _():
        o_ref[...] = jnp.zeros_like(o_ref)

        @pl.when(i == 0)
        def _():
            wg_scr[...] = jnp.zeros_like(wg_scr)

        @pl.when(i > 0)
        def _():
            ids_rows[...] = ids_t[...].T
            gates_rows[...] = gates_t[...].T
            sub = lax.broadcasted_iota(jnp.int32, (PEER_N_KEYS, PEER_SEL), 0)

            def gate_tile(t):
                ids = ids_rows[pl.ds(t, 1), :]
                gates = gates_rows[pl.ds(t, 1), :] * 0.5
                left = jnp.where(sub == (ids // PEER_N_KEYS), gates, 0.0).astype(BF16)
                right = jnp.where(sub == (ids % PEER_N_KEYS), 1.0, 0.0).astype(BF16)
                return lax.dot_general(left, right, (((1,), (1,)), ((), ())),
                                       preferred_element_type=F32)

            def pair_body(t, carry):
                packed = pltpu.pack_elementwise([gate_tile(t), gate_tile(t + half)],
                                                packed_dtype=jnp.bfloat16)
                wg_scr[pl.ds(t, PEER_N_KEYS, stride=pitch), :] = packed
                return carry

            lax.fori_loop(0, half, pair_body, 0, unroll=GATE_BUILD_UNROLL)

        for j in range(ups):
            _route_scores(j, n_groups, q_ref, sk_ref, s_scr.at[0, j])

    slot = e % 2
    a = lax.dot_general(hb_ref[...], u_ref[...], (((1,), (1,)), ((), ())),
                        preferred_element_type=F32)
    first_done = None
    for j in range(ups):
        after = a[tt - 8:tt, eb - PEER_N_KEYS:eb] if j > 0 else None
        for thunk in _route_select_steps(e * ups + j, n_groups, s_scr.at[slot, j], sv_scr.at[j],
                                         si_scr.at[j], fv_scr.at[j], fi_scr.at[j], ids_t, gates_t,
                                         after=after):
            out = thunk()
            if j == 0 and out is not None:
                first_done = out
    first_zero = _exact_zero(first_done)
    for c in range(n_sub):
        start = pl.multiple_of((e * n_sub + c) * pitch, 8)
        packed = wg_scr[pl.ds(start, half), :]
        wg = jnp.concatenate(
            [pltpu.unpack_elementwise(packed, index=k, packed_dtype=jnp.bfloat16, unpacked_dtype=F32)
             for k in range(2)], axis=0)
        if c == 0:
            wg = wg + first_zero
        ac = a[:, c * PEER_N_KEYS:(c + 1) * PEER_N_KEYS]
        gact_scr[:, c * PEER_N_KEYS:(c + 1) * PEER_N_KEYS] = (_gelu_exact_x2(ac) * wg).astype(BF16)
    o_ref[...] += jnp.dot(gact_scr[...], v_ref[...], preferred_element_type=F32)
    nxt = jnp.minimum(e + 1, pl.num_programs(1) - 1)
    for j in range(ups):
        _route_scores(nxt * ups + j, n_groups, q_ref, sk_ref, s_scr.at[1 - slot, j])

    @pl.when(e == pl.num_programs(1) - 1)
    def _():
        o_ref[...] = _layer_norm(DEEPNORM_ALPHA * h2_ref[...] + o_ref[...], g_ref[...], b_ref[...])


def _experts(q, sk_bf16, h2b, h2, u_bf16, v_bf16, g, b, tt, ups):
    t = h2b.shape[0]
    n_tiles = t // tt
    n_units = PEER_HEADS * (tt // 128)
    n_steps = n_units // ups
    eb = PEER_N_EXPERTS // n_steps
    pitch = tt // 2 + GATE_PITCH_PAD
    qw = PEER_HEADS * 2 * PEER_HALF
    once = dict(pipeline_mode=pl.Buffered(1))
    prev = lambda i, e: (jnp.maximum(i - 1, 0), 0)
    return pl.pallas_call(
        functools.partial(_expert_kernel, tt=tt, eb=eb, ups=ups),
        out_shape=jax.ShapeDtypeStruct((t, D_MODEL), F32),
        grid=(n_tiles + 1, n_steps),
        in_specs=[pl.BlockSpec((tt, qw), lambda i, e: (jnp.minimum(i, n_tiles - 1), 0), **once),
                  pl.BlockSpec((PEER_HEADS, 2, PEER_N_KEYS, PEER_HALF), lambda i, e: (0, 0, 0, 0),
                               **once),
                  pl.BlockSpec((tt, D_MODEL), prev, **once),
                  pl.BlockSpec((tt, D_MODEL), prev, **once),
                  pl.BlockSpec((eb, D_MODEL), lambda i, e: (e, 0)),
                  pl.BlockSpec((eb, D_MODEL), lambda i, e: (e, 0)),
                  pl.BlockSpec((1, D_MODEL), lambda i, e: (0, 0)),
                  pl.BlockSpec((1, D_MODEL), lambda i, e: (0, 0))],
        out_specs=pl.BlockSpec((tt, D_MODEL), prev),
        scratch_shapes=[pltpu.VMEM((PEER_N_KEYS * pitch, PEER_N_KEYS), jnp.uint32),
                        pltpu.VMEM((tt, eb), BF16),
                        pltpu.VMEM((PEER_SEL, tt), jnp.int32),
                        pltpu.VMEM((PEER_SEL, tt), F32),
                        pltpu.VMEM((tt, PEER_SEL), jnp.int32),
                        pltpu.VMEM((tt, PEER_SEL), F32),
                        pltpu.VMEM((2, ups, 2, PEER_N_KEYS, 128), F32),
                        pltpu.VMEM((ups, 2, PEER_TOPK, 128), F32),
                        pltpu.VMEM((ups, 2, PEER_TOPK, 128), jnp.int32),
                        pltpu.VMEM((ups, PEER_TOPK, 128), F32),
                        pltpu.VMEM((ups, PEER_TOPK, 128), jnp.int32)],
        compiler_params=pltpu.CompilerParams(dimension_semantics=("arbitrary", "arbitrary"),
                                             vmem_limit_bytes=EXPERT_VMEM_LIMIT_BYTES),
        name="experts",
    )(q, sk_bf16, h2b, h2, u_bf16, v_bf16, g, b)


def _rope_tables(n_slots, head_dim, lanes):
    inv = ROPE_THETA ** (-jnp.arange(0, head_dim, 2, dtype=F32) / head_dim)
    pos = jnp.arange(n_slots, dtype=jnp.int32) - PAD
    ang = pos.astype(F32)[:, None] * inv[None, :]
    cos, sin = jnp.cos(ang), jnp.sin(ang)
    reps = lanes // head_dim
    cos_t = jnp.tile(jnp.concatenate([cos, cos], axis=1), (1, reps))
    sin_t = jnp.tile(jnp.concatenate([-sin, sin], axis=1), (1, reps))
    return cos_t, sin_t


def _decay_tables():
    log_gamma = jnp.log(1.0 - 2.0 ** (-5.0 - jnp.arange(RET_HEADS, dtype=F32)))
    idx = jnp.arange(BLOCK, dtype=F32)
    rel = idx[:, None] - idx[None, :]
    inner = jnp.where(rel[None] >= 0,
                      jnp.exp(jnp.maximum(rel, 0.0)[None] * log_gamma[:, None, None]), 0.0)
    qdec = jnp.exp((idx[None, :] + 1.0) * log_gamma[:, None])[:, :, None]
    kdec = jnp.exp((BLOCK - 1.0 - idx[None, :]) * log_gamma[:, None])[:, :, None]
    chunk = jnp.exp(BLOCK * log_gamma)
    return inner, qdec, kdec, chunk


def _tile(n, pref):
    return pref if n % pref == 0 else n


def kernel(x, meta_tokens, ln_in_g, ln_in_b, w_in, att_sinks, w_out, ln_mix_g, ln_mix_b,
           peer_wq, peer_subkeys, peer_u, peer_v, ln_ffn_g, ln_ffn_b):
    batch, seq, d = x.shape
    assert d == D_MODEL and seq % BLOCK == 0 and w_in.shape[0] == DEPTH
    nbk = seq // BLOCK
    t = batch * seq
    row = lambda v: v.reshape(1, D_MODEL).astype(F32)

    x2d = x.reshape(t, D_MODEL)
    x_meta = jnp.concatenate([jnp.zeros((PAD, D_MODEL), x.dtype), meta_tokens.astype(x.dtype)], axis=0)
    kv0, kv1 = ATT_WIDTH, ATT_WIDTH + 2 * ATT_KV_WIDTH
    w_in_b = jnp.concatenate([w_in[0, :, :kv0], w_in[0, :, kv1:], w_in[0, :, kv0:kv1]],
                             axis=1).astype(BF16)

    h, proj = _inproj(x2d, row(ln_in_g), row(ln_in_b), w_in_b, _tile(t, INPROJ_TM))
    _, proj_meta = _inproj(x_meta, row(ln_in_g), row(ln_in_b), w_in_b, BLOCK)

    n_slots = seq + BLOCK
    cos_a, sin_a = _rope_tables(n_slots, ATT_HEAD_DIM, 128)
    cos_r, sin_r = _rope_tables(n_slots, RET_HEAD_DIM, RET_HEAD_DIM)
    inner, qdec, kdec, chunk = _decay_tables()

    att = _attention(proj, proj_meta, att_sinks[0].astype(F32), cos_a, sin_a, batch, nbk)
    ret = _retention(proj, proj_meta, cos_r, sin_r, inner, qdec, kdec, chunk, batch, nbk)

    h2, h2b, q = _mix(att, ret, h, w_out[0].astype(BF16), peer_wq[0].astype(BF16),
                      row(ln_mix_g[0]), row(ln_mix_b[0]), _tile(t, MIX_TM))

    out = _experts(q, peer_subkeys[0].astype(BF16), h2b, h2, peer_u[0].astype(BF16),
                   peer_v[0].astype(BF16), row(ln_ffn_g[0]), row(ln_ffn_b[0]),
                   _tile(t, EXPERT_TT), EXPERT_UNITS_PER_STEP)
    return out.reshape(batch, seq, D_MODEL)
```

```python
import functools

import jax
import jax.numpy as jnp
from jax import lax
from jax.experimental import pallas as pl
from jax.experimental.pallas import tpu as pltpu

F32 = jnp.float32
BF16 = jnp.bfloat16

D_MODEL = 2048
DEPTH = 1
N_META = 16
BLOCK = 128
PAD = BLOCK - N_META

ATT_WIDTH = 1024
ATT_HEAD_DIM = 64
ATT_Q_HEADS = 16
ATT_KV_HEADS = 2
ATT_GROUP = 8
ATT_KV_WIDTH = 128
ROPE_THETA = 10000.0

RET_WIDTH = 1024
RET_HEAD_DIM = 256
RET_HEADS = 4

IN_COLS = ATT_WIDTH + 2 * ATT_KV_WIDTH + 4 * RET_WIDTH
COL_RQ, COL_RK, COL_RV, COL_RG = 1, 2, 3, 4
COL_AK = (ATT_WIDTH + 4 * RET_WIDTH) // ATT_KV_WIDTH
COL_AV = COL_AK + 1

PEER_HEADS = 8
PEER_N_KEYS = 128
PEER_N_EXPERTS = PEER_N_KEYS * PEER_N_KEYS
PEER_HALF = 128
PEER_TOPK = 16
PEER_SEL = PEER_HEADS * PEER_TOPK

DEEPNORM_ALPHA = (2 * DEPTH) ** 0.25
LN_EPS = 1e-5
GN_EPS = 1e-5
NEG = -1e30

VMEM_LIMIT_BYTES = 56 * 1024 * 1024
EXPERT_VMEM_LIMIT_BYTES = 62 * 1024 * 1024

INPROJ_TM = 512
INPROJ_TN = 768
MIX_TM = 512
EXPERT_TT = 512
EXPERT_UNITS_PER_STEP = 2
GATE_PITCH_PAD = 8
GATE_BUILD_UNROLL = 32


def _layer_norm(x, g, b):
    mu = jnp.mean(x, axis=-1, keepdims=True)
    xc = x - mu
    var = jnp.mean(xc * xc, axis=-1, keepdims=True)
    return xc * lax.rsqrt(var + LN_EPS) * g + b


def _cparams(sem):
    return pltpu.CompilerParams(dimension_semantics=sem, vmem_limit_bytes=VMEM_LIMIT_BYTES)


def _inproj_kernel(x_ref, g_ref, b_ref, w_ref, h_ref, proj_ref):
    h = _layer_norm(x_ref[...], g_ref[...], b_ref[...])
    h_ref[...] = h
    hb = h.astype(BF16)
    for c in range(IN_COLS // INPROJ_TN):
        cols = slice(c * INPROJ_TN, (c + 1) * INPROJ_TN)
        proj_ref[:, cols] = jnp.dot(hb, w_ref[:, cols], preferred_element_type=F32).astype(BF16)


def _inproj(x2d, g, b, w_bf16, tm):
    t = x2d.shape[0]
    return pl.pallas_call(
        _inproj_kernel,
        out_shape=(jax.ShapeDtypeStruct((t, D_MODEL), F32),
                   jax.ShapeDtypeStruct((t, IN_COLS), BF16)),
        grid=(t // tm,),
        in_specs=[pl.BlockSpec((tm, D_MODEL), lambda i: (i, 0)),
                  pl.BlockSpec((1, D_MODEL), lambda i: (0, 0)),
                  pl.BlockSpec((1, D_MODEL), lambda i: (0, 0)),
                  pl.BlockSpec((D_MODEL, IN_COLS), lambda i: (0, 0), pipeline_mode=pl.Buffered(1))],
        out_specs=(pl.BlockSpec((tm, D_MODEL), lambda i: (i, 0)),
                   pl.BlockSpec((tm, IN_COLS), lambda i: (i, 0))),
        compiler_params=_cparams(("parallel",)),
        name="inproj",
    )(x2d, g, b, w_bf16)


def _att_rope(x, cos, sin_signed, lo_half):
    partner = jnp.where(lo_half, pltpu.roll(x, 96, 1), pltpu.roll(x, 32, 1))
    return x * cos + partner * sin_signed


def _att_kernel(sink_ref, q_ref, kc_ref, vc_ref, kp_ref, vp_ref, km_ref, vm_ref,
                cq_ref, sq_ref, cp_ref, sp_ref, o_ref):
    first = pl.program_id(1) == 0
    lane = lax.broadcasted_iota(jnp.int32, (BLOCK, 128), 1)
    lo_half = (lane % ATT_HEAD_DIM) < (ATT_HEAD_DIM // 2)
    cq, sq = cq_ref[...], sq_ref[...]

    k_prev = jnp.where(first, km_ref[...], kp_ref[...]).astype(F32)
    v_prev = jnp.where(first, vm_ref[...], vp_ref[...])
    k_prev = _att_rope(k_prev, cp_ref[...], sp_ref[...], lo_half)
    k_cur = _att_rope(kc_ref[...].astype(F32), cq, sq, lo_half)
    k_all = jnp.concatenate([k_prev, k_cur], axis=0).astype(BF16)
    v_all = jnp.concatenate([v_prev, vc_ref[...]], axis=0)

    qi = lax.broadcasted_iota(jnp.int32, (BLOCK, 2 * BLOCK), 0)
    kj = lax.broadcasted_iota(jnp.int32, (BLOCK, 2 * BLOCK), 1)
    prev_ok = (kj > qi) & (jnp.logical_not(first) | (kj >= PAD))
    mask = ((kj < BLOCK) & prev_ok) | ((kj >= BLOCK) & ((kj - BLOCK) <= qi))

    ones = jnp.ones((2 * BLOCK, ATT_HEAD_DIM), BF16)
    for g in range(ATT_KV_HEADS):
        heads = range(g * ATT_GROUP, (g + 1) * ATT_GROUP)
        q_parts = []
        for c in range(g * ATT_GROUP // 2, (g + 1) * ATT_GROUP // 2):
            qc = _att_rope(q_ref[:, c * 128:(c + 1) * 128].astype(F32), cq, sq, lo_half).astype(BF16)
            q_parts += [qc[:, :ATT_HEAD_DIM], qc[:, ATT_HEAD_DIM:]]
        kg = k_all[:, g * ATT_HEAD_DIM:(g + 1) * ATT_HEAD_DIM]
        vg = v_all[:, g * ATT_HEAD_DIM:(g + 1) * ATT_HEAD_DIM]
        s_all = lax.dot_general(jnp.concatenate(q_parts, axis=0), kg, (((1,), (1,)), ((), ())),
                                preferred_element_type=F32)
        p_parts, sink_terms = [], []
        for j, h in enumerate(heads):
            s = s_all[j * BLOCK:(j + 1) * BLOCK] * (ATT_HEAD_DIM ** -0.5)
            s = jnp.where(mask, s, NEG)
            sink = sink_ref[h]
            m = jnp.maximum(jnp.max(s, axis=-1, keepdims=True), sink)
            p_parts.append(jnp.exp(s - m).astype(BF16))
            sink_terms.append(jnp.exp(sink - m))
        p_all = jnp.concatenate(p_parts, axis=0)
        o_all = jnp.dot(p_all, vg, preferred_element_type=F32)
        sum_all = jnp.dot(p_all, ones, preferred_element_type=F32)
        for j, h in enumerate(heads):
            rows = slice(j * BLOCK, (j + 1) * BLOCK)
            o = o_all[rows] / (sum_all[rows] + sink_terms[j])
            o_ref[:, h * ATT_HEAD_DIM:(h + 1) * ATT_HEAD_DIM] = o.astype(BF16)


def _attention(proj, proj_meta, sinks, cos_a, sin_a, batch, nbk):
    t = proj.shape[0]

    def row(b, n):
        return b * nbk + n

    def prow(b, n):
        return b * nbk + jnp.maximum(n - 1, 0)

    blk = (BLOCK, ATT_KV_WIDTH)
    return pl.pallas_call(
        _att_kernel,
        out_shape=jax.ShapeDtypeStruct((t, ATT_WIDTH), BF16),
        grid=(batch, nbk),
        in_specs=[pl.BlockSpec(memory_space=pltpu.SMEM),
                  pl.BlockSpec((BLOCK, ATT_WIDTH), lambda b, n: (row(b, n), 0)),
                  pl.BlockSpec(blk, lambda b, n: (row(b, n), COL_AK)),
                  pl.BlockSpec(blk, lambda b, n: (row(b, n), COL_AV)),
                  pl.BlockSpec(blk, lambda b, n: (prow(b, n), COL_AK)),
                  pl.BlockSpec(blk, lambda b, n: (prow(b, n), COL_AV)),
                  pl.BlockSpec(blk, lambda b, n: (0, COL_AK)),
                  pl.BlockSpec(blk, lambda b, n: (0, COL_AV)),
                  pl.BlockSpec(blk, lambda b, n: (n + 1, 0)),
                  pl.BlockSpec(blk, lambda b, n: (n + 1, 0)),
                  pl.BlockSpec(blk, lambda b, n: (n, 0)),
                  pl.BlockSpec(blk, lambda b, n: (n, 0))],
        out_specs=pl.BlockSpec((BLOCK, ATT_WIDTH), lambda b, n: (row(b, n), 0)),
        compiler_params=_cparams(("parallel", "arbitrary")),
        name="attention",
    )(sinks, proj, proj, proj, proj, proj, proj_meta, proj_meta, cos_a, sin_a, cos_a, sin_a)


def _ret_rope(x, cos, sin_signed):
    partner = jnp.concatenate([x[:, RET_HEAD_DIM // 2:], x[:, :RET_HEAD_DIM // 2]], axis=1)
    return x * cos + partner * sin_signed


def _ret_kernel(cd_ref, q_ref, k_ref, v_ref, g_ref, km_ref, vm_ref, cq_ref, sq_ref, cm_ref, sm_ref,
                inner_ref, qd_ref, kd_ref, o_ref, state_ref):
    kscale = RET_HEAD_DIM ** -0.5
    cq, sq = cq_ref[...], sq_ref[...]

    @pl.when(pl.program_id(1) == 0)
    def _():
        valid = lax.broadcasted_iota(jnp.int32, (BLOCK, 1), 0) >= PAD
        cm, sm = cm_ref[...], sm_ref[...]
        for h in range(RET_HEADS):
            cols = slice(h * RET_HEAD_DIM, (h + 1) * RET_HEAD_DIM)
            km = _ret_rope(jnp.where(valid, km_ref[:, cols].astype(F32), 0.0), cm, sm) * kscale
            state_ref[h] = lax.dot_general((km * kd_ref[h]).astype(BF16), vm_ref[:, cols],
                                           (((0,), (0,)), ((), ())), preferred_element_type=F32)

    for h in range(RET_HEADS):
        cols = slice(h * RET_HEAD_DIM, (h + 1) * RET_HEAD_DIM)
        q = _ret_rope(q_ref[:, cols].astype(F32), cq, sq)
        k = _ret_rope(k_ref[:, cols].astype(F32), cq, sq) * kscale
        vb = v_ref[:, cols]
        state = state_ref[h]

        att = lax.dot_general(q.astype(BF16), k.astype(BF16), (((1,), (1,)), ((), ())),
                              preferred_element_type=F32) * inner_ref[h]
        o = (jnp.dot(att.astype(BF16), vb, preferred_element_type=F32)
             + jnp.dot((q * qd_ref[h]).astype(BF16), state.astype(BF16),
                       preferred_element_type=F32))
        state_ref[h] = state * cd_ref[h] + lax.dot_general(
            (k * kd_ref[h]).astype(BF16), vb, (((0,), (0,)), ((), ())), preferred_element_type=F32)

        mu = jnp.mean(o, axis=-1, keepdims=True)
        oc = o - mu
        var = jnp.mean(oc * oc, axis=-1, keepdims=True)
        gate = g_ref[:, cols].astype(F32)
        silu = gate * (1.0 / (1.0 + jnp.exp(-gate)))
        o_ref[:, cols] = (silu * (oc * lax.rsqrt(var + GN_EPS))).astype(BF16)


def _retention(proj, proj_meta, cos_r, sin_r, inner, qdec, kdec, chunk_decay, batch, nbk):
    t = proj.shape[0]
    blk = (BLOCK, RET_WIDTH)
    tab = (BLOCK, RET_HEAD_DIM)

    def row(b, n):
        return b * nbk + n

    return pl.pallas_call(
        _ret_kernel,
        out_shape=jax.ShapeDtypeStruct((t, RET_WIDTH), BF16),
        grid=(batch, nbk),
        in_specs=[pl.BlockSpec(memory_space=pltpu.SMEM),
                  pl.BlockSpec(blk, lambda b, n: (row(b, n), COL_RQ)),
                  pl.BlockSpec(blk, lambda b, n: (row(b, n), COL_RK)),
                  pl.BlockSpec(blk, lambda b, n: (row(b, n), COL_RV)),
                  pl.BlockSpec(blk, lambda b, n: (row(b, n), COL_RG)),
                  pl.BlockSpec(blk, lambda b, n: (0, COL_RK)),
                  pl.BlockSpec(blk, lambda b, n: (0, COL_RV)),
                  pl.BlockSpec(tab, lambda b, n: (n + 1, 0)),
                  pl.BlockSpec(tab, lambda b, n: (n + 1, 0)),
                  pl.BlockSpec(tab, lambda b, n: (0, 0)),
                  pl.BlockSpec(tab, lambda b, n: (0, 0)),
                  pl.BlockSpec((RET_HEADS, BLOCK, BLOCK), lambda b, n: (0, 0, 0)),
                  pl.BlockSpec((RET_HEADS, BLOCK, 1), lambda b, n: (0, 0, 0)),
                  pl.BlockSpec((RET_HEADS, BLOCK, 1), lambda b, n: (0, 0, 0))],
        out_specs=pl.BlockSpec(blk, lambda b, n: (row(b, n), 0)),
        scratch_shapes=[pltpu.VMEM((RET_HEADS, RET_HEAD_DIM, RET_HEAD_DIM), F32)],
        compiler_params=_cparams(("parallel", "arbitrary")),
        name="retention",
    )(chunk_decay, proj, proj, proj, proj, proj_meta, proj_meta, cos_r, sin_r, cos_r, sin_r,
      inner, qdec, kdec)


def _mix_kernel(att_ref, ret_ref, h_ref, w_ref, wq_ref, g_ref, b_ref, h2_ref, h2b_ref, q_ref):
    mix = (jnp.dot(att_ref[...], w_ref[:ATT_WIDTH, :], preferred_element_type=F32)
           + jnp.dot(ret_ref[...], w_ref[ATT_WIDTH:, :], preferred_element_type=F32))
    h2 = _layer_norm(DEEPNORM_ALPHA * h_ref[...] + mix, g_ref[...], b_ref[...])
    h2_ref[...] = h2
    h2b = h2.astype(BF16)
    h2b_ref[...] = h2b
    q_ref[...] = jnp.dot(h2b, wq_ref[...], preferred_element_type=F32).astype(BF16)


def _mix(att, ret, h, w_bf16, wq_bf16, g, b, tm):
    t = h.shape[0]
    qw = PEER_HEADS * 2 * PEER_HALF
    once = dict(pipeline_mode=pl.Buffered(1))
    return pl.pallas_call(
        _mix_kernel,
        out_shape=(jax.ShapeDtypeStruct((t, D_MODEL), F32),
                   jax.ShapeDtypeStruct((t, D_MODEL), BF16),
                   jax.ShapeDtypeStruct((t, qw), BF16)),
        grid=(t // tm,),
        in_specs=[pl.BlockSpec((tm, ATT_WIDTH), lambda i: (i, 0)),
                  pl.BlockSpec((tm, RET_WIDTH), lambda i: (i, 0)),
                  pl.BlockSpec((tm, D_MODEL), lambda i: (i, 0)),
                  pl.BlockSpec((D_MODEL, D_MODEL), lambda i: (0, 0), **once),
                  pl.BlockSpec((D_MODEL, qw), lambda i: (0, 0), **once),
                  pl.BlockSpec((1, D_MODEL), lambda i: (0, 0)),
                  pl.BlockSpec((1, D_MODEL), lambda i: (0, 0))],
        out_specs=(pl.BlockSpec((tm, D_MODEL), lambda i: (i, 0)),
                   pl.BlockSpec((tm, D_MODEL), lambda i: (i, 0)),
                   pl.BlockSpec((tm, qw), lambda i: (i, 0))),
        compiler_params=_cparams(("parallel",)),
        name="mix",
    )(att, ret, h, w_bf16, wq_bf16, g, b)


def _top16_steps(load, emit):
    big = jnp.iinfo(jnp.int32).max
    st = {}

    def step(k):
        if k == 0:
            st["vals"], st["order"] = load()
        vals, order = st["vals"], st["order"]
        m = jnp.max(vals, axis=0, keepdims=True)
        idx = jnp.min(jnp.where(vals == m, order, big), axis=0, keepdims=True)
        picked = order == idx
        emit(k, m, idx, picked)
        st["vals"] = jnp.where(picked, -jnp.inf, vals)
        return m

    return [functools.partial(step, k) for k in range(PEER_TOPK)]


def _top16_key_steps(load, emit):
    big = jnp.iinfo(jnp.int32).max
    n_groups = PEER_N_KEYS // 8
    st = {}

    def step(k):
        if k == 0:
            tile = load()
            st["vals"] = [tile[8 * v:8 * (v + 1)] for v in range(n_groups)]
        vals = st["vals"]
        sub = lax.broadcasted_iota(jnp.int32, vals[0].shape, 0)
        m = jnp.max(functools.reduce(jnp.maximum, vals), axis=0, keepdims=True)
        cands = [jnp.where(vals[v] == m, sub + 8 * v, big) for v in range(n_groups)]
        idx = jnp.min(functools.reduce(jnp.minimum, cands), axis=0, keepdims=True)
        emit(k, m, idx, None)
        st["vals"] = [jnp.where(sub == idx - 8 * v, -jnp.inf, vals[v]) for v in range(n_groups)]
        return m

    return [functools.partial(step, k) for k in range(PEER_TOPK)]


def _candidate_groups(sv_ref, si_ref):
    sv0, sv1, si0, si1 = sv_ref.at[0], sv_ref.at[1], si_ref.at[0], si_ref.at[1]
    row8 = lax.broadcasted_iota(jnp.int32, (8, sv_ref.shape[-1]), 0)
    vals, flat, ids = [], [], []

    def add(v, f, i, keep=None):
        vals.append(v if keep is None else jnp.where(keep, v, -jnp.inf))
        flat.append(f)
        ids.append(i)

    for a in range(2):
        for b0 in range(0, PEER_TOPK // (a + 1), 8):
            add(sv0[a:a + 1] + sv1[b0:b0 + 8], a * PEER_TOPK + b0 + row8,
                si0[a:a + 1] * PEER_N_KEYS + si1[b0:b0 + 8])
    add(sv0[8:16] + sv1[0:1], (row8 + 8) * PEER_TOPK, si0[8:16] * PEER_N_KEYS + si1[0:1])
    for b in range(5):
        a_hi = PEER_TOPK // (b + 1) - 1
        add(sv0[0:8] + sv1[b:b + 1], row8 * PEER_TOPK + b, si0[0:8] * PEER_N_KEYS + si1[b:b + 1],
            keep=(row8 >= 2) & (row8 <= a_hi))
    return (jnp.concatenate(vals, axis=0), jnp.concatenate(flat, axis=0),
            jnp.concatenate(ids, axis=0))


def _route_scores(unit, n_groups, q_ref, sk_ref, s_ref):
    h = unit // n_groups
    tok = pl.multiple_of((unit % n_groups) * 128, 128)
    for p in range(2):
        off = pl.multiple_of(h * (2 * PEER_HALF) + p * PEER_HALF, PEER_HALF)
        qhp = q_ref[pl.ds(tok, 128), pl.ds(off, PEER_HALF)]
        s_ref[p] = lax.dot_general(sk_ref[h, p], qhp, (((1,), (1,)), ((), ())),
                                   preferred_element_type=F32)


def _exact_zero(x):
    bits = pltpu.bitcast(x, jnp.uint32)
    return lax.shift_right_logical(lax.shift_right_logical(bits, jnp.uint32(16)),
                                   jnp.uint32(16)).astype(F32)


def _route_select_steps(unit, n_groups, s_ref, sv_scr, si_scr, fv_scr, fi_scr, ids_t, gates_t,
                        after=None):
    h = unit // n_groups
    tok = pl.multiple_of((unit % n_groups) * 128, 128)
    steps = []
    for p in range(2):
        def load1(p=p):
            if after is None:
                return s_ref[p]
            return s_ref[p] + jnp.tile(_exact_zero(after), (PEER_N_KEYS // 8, 1))

        def emit1(k, m, idx, picked, p=p):
            sv_scr[p, k:k + 1, :] = m
            si_scr[p, k:k + 1, :] = idx

        steps += _top16_key_steps(load1, emit1)

    st = {}

    def load2():
        cand, flat, st["idc"] = _candidate_groups(sv_scr, si_scr)
        return cand, flat

    def emit2(k, m, idx, picked):
        fv_scr[k:k + 1, :] = m
        fi_scr[k:k + 1, :] = jnp.max(jnp.where(picked, st["idc"], -1), axis=0, keepdims=True)

    steps += _top16_steps(load2, emit2)

    def finish():
        base = pl.multiple_of(h * PEER_TOPK, PEER_TOPK)
        ids_t[pl.ds(base, PEER_TOPK), pl.ds(tok, 128)] = fi_scr[...]
        fv = fv_scr[...]
        ex = jnp.exp(fv - fv[0:1, :])
        gates_t[pl.ds(base, PEER_TOPK), pl.ds(tok, 128)] = ex / jnp.sum(ex, axis=0, keepdims=True)

    return steps + [finish]


def _gelu_exact_x2(x):
    return x * (1.0 + lax.erf(x * (2.0 ** -0.5)))


def _expert_kernel(q_ref, sk_ref, hb_ref, h2_ref, u_ref, v_ref, g_ref, b_ref, o_ref,
                   wg_scr, gact_scr, ids_t, gates_t, ids_rows, gates_rows,
                   s_scr, sv_scr, si_scr, fv_scr, fi_scr, *, tt, eb, ups):
    i = pl.program_id(0)
    e = pl.program_id(1)
    half = tt // 2
    pitch = half + GATE_PITCH_PAD
    n_sub = eb // PEER_N_KEYS
    n_groups = tt // 128

    @pl.when(e == 0)
    def _():
        o_ref[...] = jnp.zeros_like(o_ref)

        @pl.when(i == 0)
        def _():
            wg_scr[...] = jnp.zeros_like(wg_scr)

        @pl.when(i > 0)
        def _():
            ids_rows[...] = ids_t[...].T
            gates_rows[...] = gates_t[...].T
            sub = lax.broadcasted_iota(jnp.int32, (PEER_N_KEYS, PEER_SEL), 0)

            def gate_tile(t):
                ids = ids_rows[pl.ds(t, 1), :]
                gates = gates_rows[pl.ds(t, 1), :] * 0.5
                left = jnp.where(sub == (ids // PEER_N_KEYS), gates, 0.0).astype(BF16)
                right = jnp.where(sub == (ids % PEER_N_KEYS), 1.0, 0.0).astype(BF16)
                return lax.dot_general(left, right, (((1,), (1,)), ((), ())),
                                       preferred_element_type=F32)

            def pair_body(t, carry):
                packed = pltpu.pack_elementwise([gate_tile(t), gate_tile(t + half)],
                                                packed_dtype=jnp.bfloat16)
                wg_scr[pl.ds(t, PEER_N_KEYS, stride=pitch), :] = packed
                return carry

            lax.fori_loop(0, half, pair_body, 0, unroll=GATE_BUILD_UNROLL)

        for j in range(ups):
            _route_scores(j, n_groups, q_ref, sk_ref, s_scr.at[0, j])
        gact_scr[1] = jnp.zeros(gact_scr.shape[1:], BF16)

    last = pl.num_programs(1) - 2
    slot = e % 2
    eu = jnp.minimum(e, last)
    a = lax.dot_general(hb_ref[...], u_ref[...], (((1,), (1,)), ((), ())),
                        preferred_element_type=F32)
    o_ref[...] += jnp.dot(gact_scr[1 - slot], v_ref[...], preferred_element_type=F32)
    first_done = None
    for j in range(ups):
        after = a[tt - 8:tt, eb - PEER_N_KEYS:eb] if j > 0 else None
        for thunk in _route_select_steps(eu * ups + j, n_groups, s_scr.at[slot, j], sv_scr.at[j],
                                         si_scr.at[j], fv_scr.at[j], fi_scr.at[j], ids_t, gates_t,
                                         after=after):
            out = thunk()
            if j == 0 and out is not None:
                first_done = out
    first_zero = _exact_zero(first_done)
    for c in range(n_sub):
        start = pl.multiple_of((eu * n_sub + c) * pitch, 8)
        packed = wg_scr[pl.ds(start, half), :]
        wg = jnp.concatenate(
            [pltpu.unpack_elementwise(packed, index=k, packed_dtype=jnp.bfloat16, unpacked_dtype=F32)
             for k in range(2)], axis=0)
        if c == 0:
            wg = wg + first_zero
        ac = a[:, c * PEER_N_KEYS:(c + 1) * PEER_N_KEYS]
        gact_scr[slot, :, c * PEER_N_KEYS:(c + 1) * PEER_N_KEYS] = (
            _gelu_exact_x2(ac) * wg).astype(BF16)
    nxt = jnp.minimum(e + 1, last)
    for j in range(ups):
        _route_scores(nxt * ups + j, n_groups, q_ref, sk_ref, s_scr.at[1 - slot, j])

    @pl.when(e == pl.num_programs(1) - 1)
    def _():
        o_ref[...] = _layer_norm(DEEPNORM_ALPHA * h2_ref[...] + o_ref[...], g_ref[...], b_ref[...])


def _experts(q, sk_bf16, h2b, h2, u_bf16, v_bf16, g, b, tt, ups):
    t = h2b.shape[0]
    n_tiles = t // tt
    n_units = PEER_HEADS * (tt // 128)
    n_steps = n_units // ups
    eb = PEER_N_EXPERTS // n_steps
    pitch = tt // 2 + GATE_PITCH_PAD
    qw = PEER_HEADS * 2 * PEER_HALF
    once = dict(pipeline_mode=pl.Buffered(1))
    prev = lambda i, e: (jnp.maximum(i - 1, 0), 0)
    return pl.pallas_call(
        functools.partial(_expert_kernel, tt=tt, eb=eb, ups=ups),
        out_shape=jax.ShapeDtypeStruct((t, D_MODEL), F32),
        grid=(n_tiles + 1, n_steps + 1),
        in_specs=[pl.BlockSpec((tt, qw), lambda i, e: (jnp.minimum(i, n_tiles - 1), 0), **once),
                  pl.BlockSpec((PEER_HEADS, 2, PEER_N_KEYS, PEER_HALF), lambda i, e: (0, 0, 0, 0),
                               **once),
                  pl.BlockSpec((tt, D_MODEL), prev, **once),
                  pl.BlockSpec((tt, D_MODEL), prev, **once),
                  pl.BlockSpec((eb, D_MODEL), lambda i, e: (jnp.minimum(e, n_steps - 1), 0)),
                  pl.BlockSpec((eb, D_MODEL), lambda i, e: (jnp.maximum(e - 1, 0), 0)),
                  pl.BlockSpec((1, D_MODEL), lambda i, e: (0, 0)),
                  pl.BlockSpec((1, D_MODEL), lambda i, e: (0, 0))],
        out_specs=pl.BlockSpec((tt, D_MODEL), prev),
        scratch_shapes=[pltpu.VMEM((PEER_N_KEYS * pitch, PEER_N_KEYS), jnp.uint32),
                        pltpu.VMEM((2, tt, eb), BF16),
                        pltpu.VMEM((PEER_SEL, tt), jnp.int32),
                        pltpu.VMEM((PEER_SEL, tt), F32),
                        pltpu.VMEM((tt, PEER_SEL), jnp.int32),
                        pltpu.VMEM((tt, PEER_SEL), F32),
                        pltpu.VMEM((2, ups, 2, PEER_N_KEYS, 128), F32),
                        pltpu.VMEM((ups, 2, PEER_TOPK, 128), F32),
                        pltpu.VMEM((ups, 2, PEER_TOPK, 128), jnp.int32),
                        pltpu.VMEM((ups, PEER_TOPK, 128), F32),
                        pltpu.VMEM((ups, PEER_TOPK, 128), jnp.int32)],
        compiler_params=pltpu.CompilerParams(dimension_semantics=("arbitrary", "arbitrary"),
                                             vmem_limit_bytes=EXPERT_VMEM_LIMIT_BYTES),
        name="experts",
    )(q, sk_bf16, h2b, h2, u_bf16, v_bf16, g, b)


def _rope_tables(n_slots, head_dim, lanes):
    inv = ROPE_THETA ** (-jnp.arange(0, head_dim, 2, dtype=F32) / head_dim)
    pos = jnp.arange(n_slots, dtype=jnp.int32) - PAD
    ang = pos.astype(F32)[:, None] * inv[None, :]
    cos, sin = jnp.cos(ang), jnp.sin(ang)
    reps = lanes // head_dim
    cos_t = jnp.tile(jnp.concatenate([cos, cos], axis=1), (1, reps))
    sin_t = jnp.tile(jnp.concatenate([-sin, sin], axis=1), (1, reps))
    return cos_t, sin_t


def _decay_tables():
    log_gamma = jnp.log(1.0 - 2.0 ** (-5.0 - jnp.arange(RET_HEADS, dtype=F32)))
    idx = jnp.arange(BLOCK, dtype=F32)
    rel = idx[:, None] - idx[None, :]
    inner = jnp.where(rel[None] >= 0,
                      jnp.exp(jnp.maximum(rel, 0.0)[None] * log_gamma[:, None, None]), 0.0)
    qdec = jnp.exp((idx[None, :] + 1.0) * log_gamma[:, None])[:, :, None]
    kdec = jnp.exp((BLOCK - 1.0 - idx[None, :]) * log_gamma[:, None])[:, :, None]
    chunk = jnp.exp(BLOCK * log_gamma)
    return inner, qdec, kdec, chunk


def _tile(n, pref):
    return pref if n % pref == 0 else n


def kernel(x, meta_tokens, ln_in_g, ln_in_b, w_in, att_sinks, w_out, ln_mix_g, ln_mix_b,
           peer_wq, peer_subkeys, peer_u, peer_v, ln_ffn_g, ln_ffn_b):
    batch, seq, d = x.shape
    assert d == D_MODEL and seq % BLOCK == 0 and w_in.shape[0] == DEPTH
    nbk = seq // BLOCK
    t = batch * seq
    row = lambda v: v.reshape(1, D_MODEL).astype(F32)

    x2d = x.reshape(t, D_MODEL)
    x_meta = jnp.concatenate([jnp.zeros((PAD, D_MODEL), x.dtype), meta_tokens.astype(x.dtype)], axis=0)
    kv0, kv1 = ATT_WIDTH, ATT_WIDTH + 2 * ATT_KV_WIDTH
    w_in_b = jnp.concatenate([w_in[0, :, :kv0], w_in[0, :, kv1:], w_in[0, :, kv0:kv1]],
                             axis=1).astype(BF16)

    h, proj = _inproj(x2d, row(ln_in_g), row(ln_in_b), w_in_b, _tile(t, INPROJ_TM))
    _, proj_meta = _inproj(x_meta, row(ln_in_g), row(ln_in_b), w_in_b, BLOCK)

    n_slots = seq + BLOCK
    cos_a, sin_a = _rope_tables(n_slots, ATT_HEAD_DIM, 128)
    cos_r, sin_r = _rope_tables(n_slots, RET_HEAD_DIM, RET_HEAD_DIM)
    inner, qdec, kdec, chunk = _decay_tables()

    att = _attention(proj, proj_meta, att_sinks[0].astype(F32), cos_a, sin_a, batch, nbk)
    ret = _retention(proj, proj_meta, cos_r, sin_r, inner, qdec, kdec, chunk, batch, nbk)

    h2, h2b, q = _mix(att, ret, h, w_out[0].astype(BF16), peer_wq[0].astype(BF16),
                      row(ln_mix_g[0]), row(ln_mix_b[0]), _tile(t, MIX_TM))

    out = _experts(q, peer_subkeys[0].astype(BF16), h2b, h2, peer_u[0].astype(BF16),
                   peer_v[0].astype(BF16), row(ln_ffn_g[0]), row(ln_ffn_b[0]),
                   _tile(t, EXPERT_TT), EXPERT_UNITS_PER_STEP)
    return out.reshape(batch, seq, D_MODEL)
```
